```python
import jax, jax.numpy as jnp
from jax import lax
import numpy as np

D_MODEL = 1024
BATCH = 4
SEQ = 4096
DEPTH = 4

HEAD_DIM = 64
N_ATT_HEADS = 8
D_ATT = N_ATT_HEADS * HEAD_DIM
N_CONV_GROUPS = 4
D_CONV = N_CONV_GROUPS * HEAD_DIM
N_SGU_GROUPS = 4
D_SGU = N_SGU_GROUPS * HEAD_DIM
SGU_CHUNK = 128
Q_BLOCK = 128
CONV_WIDTH = 3
D_FF = 2816
N_BRANCHES = 3
RMS_EPS = 1e-6
LN_EPS = 1e-5
IN_WIDTHS = (D_ATT, D_ATT, D_ATT, N_ATT_HEADS, D_CONV, D_CONV, D_CONV, D_SGU, D_SGU, N_BRANCHES * D_MODEL)
IN_WIDTH = 3 * D_ATT + N_ATT_HEADS + 3 * D_CONV + 2 * D_SGU + N_BRANCHES * D_MODEL

kernel_name = "fox_shortconv_sgu_gated_hybrid"


def rms_norm(x, g):
    xf = x.astype(jnp.float32)
    y = xf * lax.rsqrt(jnp.mean(xf * xf, axis=-1, keepdims=True) + RMS_EPS)
    return (y * g.astype(jnp.float32)).astype(x.dtype)


def layer_norm(x, g, b):
    xf = x.astype(jnp.float32)
    mu = jnp.mean(xf, axis=-1, keepdims=True)
    xc = xf - mu
    var = jnp.mean(xc * xc, axis=-1, keepdims=True)
    y = xc * lax.rsqrt(var + LN_EPS) * g.astype(jnp.float32) + b.astype(jnp.float32)
    return y.astype(x.dtype)


def split_cols(h, widths):
    offs = np.cumsum(np.array(widths))[:-1].tolist()
    return jnp.split(h, offs, axis=-1)


def causal_dwconv(x, w):
    K = w.shape[0]
    S = x.shape[1]
    xp = jnp.pad(x, ((0, 0), (K - 1, 0), (0, 0)))
    y = xp[:, K - 1:K - 1 + S] * w[K - 1]
    for k in range(K - 1):
        y = y + xp[:, k:k + S] * w[k]
    return y


def fox_attention(q, k, v, logf):
    S = q.shape[1]
    scale = HEAD_DIM ** -0.5
    c = jnp.cumsum(logf, axis=1).transpose(0, 2, 1)
    outs = []
    for i in range(S // Q_BLOCK):
        q0 = i * Q_BLOCK
        q1 = q0 + Q_BLOCK
        s = jnp.einsum('bqhd,bkhd->bhqk', q[:, q0:q1], k[:, :q1]).astype(jnp.float32) * scale
        bias = c[:, :, q0:q1, None] - c[:, :, None, :q1]
        mask = jnp.arange(q0, q1)[:, None] >= jnp.arange(q1)[None, :]
        s = jnp.where(mask, s + bias, -jnp.inf)
        p = jax.nn.softmax(s, axis=-1).astype(v.dtype)
        outs.append(jnp.einsum('bhqk,bkhd->bqhd', p, v[:, :q1]))
    return jnp.concatenate(outs, axis=1)


def short_conv_mixer(b_gate, c_gate, h, conv_w):
    return b_gate * causal_dwconv(c_gate * h, conv_w)


def chunked_sgu(u, v, ln_g, ln_b, w_s, b_s):
    B_, S, _ = u.shape
    u = jax.nn.gelu(u, approximate=True)
    v = layer_norm(jax.nn.gelu(v, approximate=True), ln_g, ln_b)
    n = S // SGU_CHUNK
    vc = v.reshape(B_, n, SGU_CHUNK, N_SGU_GROUPS, HEAD_DIM)
    mask = jnp.tril(jnp.ones((SGU_CHUNK, SGU_CHUNK), w_s.dtype))
    mixed = jnp.einsum('gts,bnsgd->bntgd', w_s * mask, vc) + b_s.T[:, :, None]
    return u * mixed.reshape(B_, S, D_SGU)


def conv_gated_ffn(x, w_up, conv_w, w_down):
    h = causal_dwconv(x @ w_up, conv_w)
    a, b = jnp.split(h, 2, axis=-1)
    return (jax.nn.gelu(a, approximate=True) * b) @ w_down


def setup_inputs(seed: int = 0) -> dict:
    key = jax.random.key(seed)
    ks = jax.random.split(key, 24)
    f32 = jnp.float32

    def nrm(k, shape, scale):
        return jax.random.normal(k, shape, f32) * scale

    L = DEPTH
    return {
        "x": nrm(ks[0], (BATCH, SEQ, D_MODEL), 1.0),
        "pre_mix_g": 1.0 + nrm(ks[1], (L, D_MODEL), 0.02),
        "post_mix_g": 1.0 + nrm(ks[2], (L, D_MODEL), 0.02),
        "pre_ffn_g": 1.0 + nrm(ks[3], (L, D_MODEL), 0.02),
        "post_ffn_g": 1.0 + nrm(ks[4], (L, D_MODEL), 0.02),
        "w_in": nrm(ks[5], (L, D_MODEL, IN_WIDTH), D_MODEL ** -0.5),
        "b_forget": jnp.linspace(1.0, 6.0, N_ATT_HEADS, dtype=f32)[None, :] + nrm(ks[6], (L, N_ATT_HEADS), 0.1),
        "b_gate": nrm(ks[7], (L, N_BRANCHES, D_MODEL), 0.02),
        "conv_mix_w": nrm(ks[8], (L, CONV_WIDTH, D_CONV), CONV_WIDTH ** -0.5),
        "sgu_ln_g": 1.0 + nrm(ks[9], (L, D_SGU), 0.02),
        "sgu_ln_b": nrm(ks[10], (L, D_SGU), 0.02),
        "sgu_w": nrm(ks[11], (L, N_SGU_GROUPS, SGU_CHUNK, SGU_CHUNK), SGU_CHUNK ** -0.5),
        "sgu_b": 1.0 + nrm(ks[12], (L, N_SGU_GROUPS, SGU_CHUNK), 0.02),
        "w_branch_att": nrm(ks[13], (L, D_ATT, D_MODEL), D_ATT ** -0.5),
        "w_branch_conv": nrm(ks[14], (L, D_CONV, D_MODEL), D_CONV ** -0.5),
        "w_branch_sgu": nrm(ks[15], (L, D_SGU, D_MODEL), D_SGU ** -0.5),
        "w_out": nrm(ks[16], (L, D_MODEL, D_MODEL), D_MODEL ** -0.5),
        "w_ffn_up": nrm(ks[17], (L, D_MODEL, 2 * D_FF), D_MODEL ** -0.5),
        "conv_ffn_w": nrm(ks[18], (L, CONV_WIDTH, 2 * D_FF), CONV_WIDTH ** -0.5),
        "w_ffn_down": nrm(ks[19], (L, D_FF, D_MODEL), D_FF ** -0.5),
    }


def reference(x, pre_mix_g, post_mix_g, pre_ffn_g, post_ffn_g, w_in, b_forget, b_gate,
              conv_mix_w, sgu_ln_g, sgu_ln_b, sgu_w, sgu_b, w_branch_att, w_branch_conv,
              w_branch_sgu, w_out, w_ffn_up, conv_ffn_w, w_ffn_down):
    B_, S, D = x.shape
    for l in range(DEPTH):
        xn = rms_norm(x, pre_mix_g[l])
        h = xn @ w_in[l]
        q, k, v, f_logit, bg, cg, hc, u, vs, g_logit = split_cols(h, IN_WIDTHS)
        q = q.reshape(B_, S, N_ATT_HEADS, HEAD_DIM)
        k = k.reshape(B_, S, N_ATT_HEADS, HEAD_DIM)
        v = v.reshape(B_, S, N_ATT_HEADS, HEAD_DIM)
        logf = jax.nn.log_sigmoid((f_logit + b_forget[l]).astype(jnp.float32))
        y_att = fox_attention(q, k, v, logf).reshape(B_, S, D_ATT) @ w_branch_att[l]
        y_conv = short_conv_mixer(bg, cg, hc, conv_mix_w[l]) @ w_branch_conv[l]
        y_sgu = chunked_sgu(u, vs, sgu_ln_g[l], sgu_ln_b[l], sgu_w[l], sgu_b[l]) @ w_branch_sgu[l]
        gates = jax.nn.sigmoid(g_logit.reshape(B_, S, N_BRANCHES, D) + b_gate[l])
        merged = gates[:, :, 0] * y_att + gates[:, :, 1] * y_conv + gates[:, :, 2] * y_sgu
        x = x + rms_norm(merged @ w_out[l], post_mix_g[l])
        xn = rms_norm(x, pre_ffn_g[l])
        x = x + rms_norm(conv_gated_ffn(xn, w_ffn_up[l], conv_ffn_w[l], w_ffn_down[l]), post_ffn_g[l])
    return x
```

```python
import functools

import jax
import jax.numpy as jnp
from jax import lax
from jax.experimental import pallas as pl
from jax.experimental.pallas import tpu as pltpu

F32 = jnp.float32
BF16 = jnp.bfloat16

HEAD_DIM = 64
N_HEADS = 8
D_ATT = N_HEADS * HEAD_DIM
D_CONV = 256
D_SGU = 256
N_SGU_GROUPS = 4
SGU_CHUNK = 128
CONV_WIDTH = 3
N_BRANCHES = 3
RMS_EPS = 1e-6
LN_EPS = 1e-5

LANES = 128
SUBLANES = 8
ROW_TILE = 512
ATT_TILE = 512
FF_CHUNK = 256
MASK_VALUE = -1e30
VMEM_LIMIT = 56 * 1024 * 1024

OFF_Q = 0
OFF_K = OFF_Q + D_ATT
OFF_V = OFF_K + D_ATT
OFF_BG = OFF_V + D_ATT
OFF_CG = OFF_BG + D_CONV
OFF_HC = OFF_CG + D_CONV
OFF_U = OFF_HC + D_CONV
OFF_VS = OFF_U + D_SGU
OFF_G = OFF_VS + D_SGU


def _rms(x, g):
    return x * lax.rsqrt(jnp.mean(x * x, axis=-1, keepdims=True) + RMS_EPS) * g


def _gelu_tanh(x):
    return 0.5 * x * (1.0 + jnp.tanh(0.7978845608028654 * (x + 0.044715 * (x * x * x))))


def _sigmoid(x):
    return 1.0 / (1.0 + jnp.exp(-x))


def _log_sigmoid(x):
    return jnp.minimum(x, 0.0) - jnp.log1p(jnp.exp(-jnp.abs(x)))


def _causal_conv3(z, prev, w):
    c = z.shape[1]
    rid = lax.broadcasted_iota(jnp.int32, (SUBLANES, c), 0)
    z1 = pltpu.roll(z, 1, 0)
    z2 = pltpu.roll(z, 2, 0)
    p1 = prev[SUBLANES - 1:SUBLANES, :]
    p2 = prev[SUBLANES - 2:SUBLANES - 1, :]
    top1 = jnp.where(rid == 0, p1, z1[0:SUBLANES])
    top2 = jnp.where(rid == 0, p2, jnp.where(rid == 1, p1, z2[0:SUBLANES]))
    z1 = jnp.concatenate([top1, z1[SUBLANES:]], axis=0)
    z2 = jnp.concatenate([top2, z2[SUBLANES:]], axis=0)
    return w[2:3, :] * z + w[1:2, :] * z1 + w[0:1, :] * z2


def _split3_bf16(c):
    hi = c.astype(BF16).astype(F32)
    r = c - hi
    mid = r.astype(BF16).astype(F32)
    lo = (r - mid).astype(BF16).astype(F32)
    return hi, mid, lo


def _in_proj_body(x_ref, g_ref, w_ref, bf_ref, bgate_ref, cw_ref, lng_ref, lnb_ref, sw_ref, sb_ref,
                  wbc_ref, wbs_ref, qa_ref, ka_ref, v_ref, g0_ref, part_ref, c_carry, z_carry):
    tm = x_ref.shape[1]

    @pl.when(pl.program_id(1) == 0)
    def _():
        c_carry[...] = jnp.zeros_like(c_carry)
        z_carry[...] = jnp.zeros_like(z_carry)

    x = x_ref[0]
    xn = _rms(x, g_ref[0]).astype(BF16)

    def proj(off, width):
        return jnp.dot(xn, w_ref[0, :, off:off + width], preferred_element_type=F32)

    logf = _log_sigmoid(proj(OFF_G + N_BRANCHES * x_ref.shape[2], LANES) + bf_ref[0])
    rows = lax.broadcasted_iota(jnp.int32, (tm, LANES), 0)
    c = logf
    sh = 1
    while sh < tm:
        c = c + jnp.where(rows >= sh, pltpu.roll(c, sh, 0), 0.0)
        sh *= 2
    c = c + c_carry[SUBLANES - 1:SUBLANES, :]
    c_carry[...] = c[tm - SUBLANES:tm, :]
    c_hi, c_mid, c_lo = _split3_bf16(c)

    q = proj(OFF_Q, D_ATT) * (HEAD_DIM ** -0.5)
    k = proj(OFF_K, D_ATT)
    v_ref[0] = proj(OFF_V, D_ATT).astype(BF16)
    lane = lax.broadcasted_iota(jnp.int32, (tm, LANES), 1)
    e = lane - HEAD_DIM
    k_ones = jnp.where((e >= 0) & (e < 3), 1.0, 0.0)
    q_ones = jnp.where((e >= 3) & (e < 6), 1.0, 0.0)
    for h in range(N_HEADS):
        pair = h // 2
        qp = q[:, pair * LANES:(pair + 1) * LANES]
        kp = k[:, pair * LANES:(pair + 1) * LANES]
        if h % 2 == 1:
            qp = pltpu.roll(qp, HEAD_DIM, 1)
            kp = pltpu.roll(kp, HEAD_DIM, 1)
        hi = c_hi[:, h:h + 1]
        mid = c_mid[:, h:h + 1]
        lo = c_lo[:, h:h + 1]
        q_extra = jnp.where(e == 0, hi, jnp.where(e == 1, mid, jnp.where(e == 2, lo, q_ones)))
        k_extra = jnp.where(e == 3, -hi, jnp.where(e == 4, -mid, jnp.where(e == 5, -lo, k_ones)))
        qa_ref[0, h] = jnp.where(lane < HEAD_DIM, qp, q_extra).astype(BF16)
        ka_ref[0, h] = jnp.where(lane < HEAD_DIM, kp, k_extra).astype(BF16)

    z = proj(OFF_CG, D_CONV) * proj(OFF_HC, D_CONV)
    conv_pre = proj(OFF_BG, D_CONV) * _causal_conv3(z, z_carry[...], cw_ref[0])
    z_carry[...] = z[tm - SUBLANES:tm, :]
    y_conv = jnp.dot(conv_pre.astype(BF16), wbc_ref[0], preferred_element_type=F32)

    u = _gelu_tanh(proj(OFF_U, D_SGU))
    vs = _gelu_tanh(proj(OFF_VS, D_SGU))
    mu = jnp.mean(vs, axis=-1, keepdims=True)
    vc = vs - mu
    var = jnp.mean(vc * vc, axis=-1, keepdims=True)
    vln = vc * lax.rsqrt(var + LN_EPS) * lng_ref[0] + lnb_ref[0]
    grp = lax.broadcasted_iota(jnp.int32, (SGU_CHUNK, D_SGU), 1) // HEAD_DIM
    mixed = []
    for ci in range(tm // SGU_CHUNK):
        vchunk = vln[ci * SGU_CHUNK:(ci + 1) * SGU_CHUNK, :]
        stacked = jnp.concatenate(
            [jnp.where(grp == gi, vchunk, 0.0).astype(BF16) for gi in range(N_SGU_GROUPS)], axis=0)
        mixed.append(jnp.dot(sw_ref[0], stacked, preferred_element_type=F32) + sb_ref[0])
    sgu_pre = u * jnp.concatenate(mixed, axis=0)
    y_sgu = jnp.dot(sgu_pre.astype(BF16), wbs_ref[0], preferred_element_type=F32)

    d = x_ref.shape[2]

    def gate(i):
        return _sigmoid(proj(OFF_G + i * d, d) + bgate_ref[0, :, i * d:(i + 1) * d])

    g0_ref[0] = gate(0).astype(BF16)
    part_ref[0] = (gate(1) * y_conv + gate(2) * y_sgu).astype(BF16)


def _in_proj(l, x, pre_mix_g, w_in_r, b_forget_p, b_gate_r, conv_mix_w, sgu_ln_g, sgu_ln_b, sgu_w_cat,
             sgu_bias, w_bc, w_bs):
    b, s, d = x.shape
    tm = min(ROW_TILE, s)
    win = w_in_r.shape[2]

    def layer(shape):
        nd = len(shape)
        return pl.BlockSpec((1,) + shape, lambda bi, ti: (l,) + (0,) * nd, pipeline_mode=pl.Buffered(1))

    return pl.pallas_call(
        _in_proj_body,
        grid=(b, s // tm),
        in_specs=[
            pl.BlockSpec((1, tm, d), lambda bi, ti: (bi, ti, 0)),
            layer((1, d)),
            layer((d, win)),
            layer((1, LANES)),
            layer((1, N_BRANCHES * d)),
            layer((CONV_WIDTH, D_CONV)),
            layer((1, D_SGU)),
            layer((1, D_SGU)),
            layer((SGU_CHUNK, N_SGU_GROUPS * SGU_CHUNK)),
            layer((SGU_CHUNK, D_SGU)),
            layer((D_CONV, d)),
            layer((D_SGU, d)),
        ],
        out_specs=[
            pl.BlockSpec((1, N_HEADS, tm, LANES), lambda bi, ti: (bi, 0, ti, 0)),
            pl.BlockSpec((1, N_HEADS, tm, LANES), lambda bi, ti: (bi, 0, ti, 0)),
            pl.BlockSpec((1, tm, D_ATT), lambda bi, ti: (bi, ti, 0)),
            pl.BlockSpec((1, tm, d), lambda bi, ti: (bi, ti, 0)),
            pl.BlockSpec((1, tm, d), lambda bi, ti: (bi, ti, 0)),
        ],
        out_shape=[
            jax.ShapeDtypeStruct((b, N_HEADS, s, LANES), BF16),
            jax.ShapeDtypeStruct((b, N_HEADS, s, LANES), BF16),
            jax.ShapeDtypeStruct((b, s, D_ATT), BF16),
            jax.ShapeDtypeStruct((b, s, d), BF16),
            jax.ShapeDtypeStruct((b, s, d), BF16),
        ],
        scratch_shapes=[pltpu.VMEM((SUBLANES, LANES), F32), pltpu.VMEM((SUBLANES, D_CONV), F32)],
        compiler_params=pltpu.CompilerParams(
            dimension_semantics=("arbitrary", "arbitrary"), vmem_limit_bytes=VMEM_LIMIT),
        name=f"in_proj_l{l}",
    )(x, pre_mix_g, w_in_r, b_forget_p, b_gate_r, conv_mix_w, sgu_ln_g, sgu_ln_b, sgu_w_cat, sgu_bias, w_bc, w_bs)


def _attn_body(q_ref, k_ref, v_ref, o_ref, m_sc, l_sc, acc_sc):
    tq = q_ref.shape[2]
    qi = pl.program_id(2)
    m_sc[...] = jnp.full_like(m_sc, MASK_VALUE)
    l_sc[...] = jnp.zeros_like(l_sc)
    acc_sc[...] = jnp.zeros_like(acc_sc)

    def step(j, masked):
        v = v_ref[0, pl.ds(pl.multiple_of(j * tq, tq), tq), :]
        for hh in range(2):
            kt = k_ref[0, hh, pl.ds(pl.multiple_of(j * tq, tq), tq), :]
            s = lax.dot_general(q_ref[0, hh], kt, (((1,), (1,)), ((), ())), preferred_element_type=F32)
            if masked:
                row = lax.broadcasted_iota(jnp.int32, (tq, tq), 0)
                col = lax.broadcasted_iota(jnp.int32, (tq, tq), 1)
                s = jnp.where(row >= col, s, MASK_VALUE)
            m_old = m_sc[hh]
            m_new = jnp.maximum(m_old, jnp.max(s, axis=-1, keepdims=True))
            alpha = jnp.exp(m_old - m_new)
            p = jnp.exp(s - m_new)
            l_sc[hh] = alpha * l_sc[hh] + jnp.sum(p, axis=-1, keepdims=True)
            acc_sc[hh] = alpha * acc_sc[hh] + jnp.dot(p.astype(BF16), v, preferred_element_type=F32)
            m_sc[hh] = m_new

    def body(j, carry):
        step(j, False)
        return carry

    lax.fori_loop(0, qi, body, 0)
    step(qi, True)
    lane = lax.broadcasted_iota(jnp.int32, (tq, LANES), 1)
    o = jnp.where(lane < HEAD_DIM, acc_sc[0] / l_sc[0], acc_sc[1] / l_sc[1])
    o_ref[0] = o.astype(BF16)


def _fox_attn(l, qa, ka, v):
    b, _, s, _ = qa.shape
    tq = min(ATT_TILE, s)
    return pl.pallas_call(
        _attn_body,
        grid=(b, N_HEADS // 2, s // tq),
        in_specs=[
            pl.BlockSpec((1, 2, tq, LANES), lambda bi, gi, qi: (bi, gi, qi, 0)),
            pl.BlockSpec((1, 2, s, LANES), lambda bi, gi, qi: (bi, gi, 0, 0)),
            pl.BlockSpec((1, s, LANES), lambda bi, gi, qi: (bi, 0, gi)),
        ],
        out_specs=pl.BlockSpec((1, tq, LANES), lambda bi, gi, qi: (bi, qi, gi)),
        out_shape=jax.ShapeDtypeStruct((b, s, D_ATT), BF16),
        scratch_shapes=[
            pltpu.VMEM((2, tq, 1), F32),
            pltpu.VMEM((2, tq, 1), F32),
            pltpu.VMEM((2, tq, LANES), F32),
        ],
        compiler_params=pltpu.CompilerParams(
            dimension_semantics=("arbitrary", "arbitrary", "arbitrary"), vmem_limit_bytes=VMEM_LIMIT),
        name=f"fox_attn_l{l}",
    )(qa, ka, v)


def _post_body(att_ref, g0_ref, part_ref, x_ref, wba_ref, wout_ref, pomg_ref, prfg_ref, wup_ref, cfw_ref,
               wdn_ref, pofg_ref, o_ref, h_carry):
    tm = x_ref.shape[1]
    d_ff = wdn_ref.shape[1]

    @pl.when(pl.program_id(1) == 0)
    def _():
        h_carry[...] = jnp.zeros_like(h_carry)

    y_att = jnp.dot(att_ref[0], wba_ref[0], preferred_element_type=F32)
    merged = g0_ref[0].astype(F32) * y_att + part_ref[0].astype(F32)
    mix = jnp.dot(merged.astype(BF16), wout_ref[0], preferred_element_type=F32)
    x1 = x_ref[0] + _rms(mix, pomg_ref[0])
    xn = _rms(x1, prfg_ref[0]).astype(BF16)

    acc = jnp.zeros((tm, x_ref.shape[2]), F32)
    for j in range(d_ff // FF_CHUNK):
        halves = []
        for off in (j * FF_CHUNK, d_ff + j * FF_CHUNK):
            h = jnp.dot(xn, wup_ref[0, :, off:off + FF_CHUNK], preferred_element_type=F32)
            halves.append(_causal_conv3(h, h_carry[:, off:off + FF_CHUNK], cfw_ref[0, :, off:off + FF_CHUNK]))
            h_carry[:, off:off + FF_CHUNK] = h[tm - SUBLANES:tm, :]
        gated = (_gelu_tanh(halves[0]) * halves[1]).astype(BF16)
        acc = acc + jnp.dot(gated, wdn_ref[0, j * FF_CHUNK:(j + 1) * FF_CHUNK, :], preferred_element_type=F32)
    o_ref[0] = x1 + _rms(acc, pofg_ref[0])


def _post(l, att, g0, part, x, w_ba, w_out, post_mix_g, pre_ffn_g, w_up, conv_ffn_w, w_down, post_ffn_g):
    b, s, d = x.shape
    tm = min(ROW_TILE, s)
    d_ff = w_down.shape[1]

    def layer(shape):
        nd = len(shape)
        return pl.BlockSpec((1,) + shape, lambda bi, ti: (l,) + (0,) * nd, pipeline_mode=pl.Buffered(1))

    def rows(width):
        return pl.BlockSpec((1, tm, width), lambda bi, ti: (bi, ti, 0))

    return pl.pallas_call(
        _post_body,
        grid=(b, s // tm),
        in_specs=[
            rows(D_ATT), rows(d), rows(d), rows(d),
            layer((D_ATT, d)),
            layer((d, d)),
            layer((1, d)),
            layer((1, d)),
            layer((d, 2 * d_ff)),
            layer((CONV_WIDTH, 2 * d_ff)),
            layer((d_ff, d)),
            layer((1, d)),
        ],
        out_specs=rows(d),
        out_shape=jax.ShapeDtypeStruct((b, s, d), F32),
        scratch_shapes=[pltpu.VMEM((SUBLANES, 2 * d_ff), F32)],
        compiler_params=pltpu.CompilerParams(
            dimension_semantics=("arbitrary", "arbitrary"), vmem_limit_bytes=VMEM_LIMIT),
        name=f"post_l{l}",
    )(att, g0, part, x, w_ba, w_out, post_mix_g, pre_ffn_g, w_up, conv_ffn_w, w_down, post_ffn_g)


def kernel(x, pre_mix_g, post_mix_g, pre_ffn_g, post_ffn_g, w_in, b_forget, b_gate, conv_mix_w, sgu_ln_g,
           sgu_ln_b, sgu_w, sgu_b, w_branch_att, w_branch_conv, w_branch_sgu, w_out, w_ffn_up, conv_ffn_w,
           w_ffn_down):
    depth, d, _ = w_in.shape
    assert d % LANES == 0 and x.shape[1] % min(ROW_TILE, x.shape[1]) == 0
    assert min(ROW_TILE, x.shape[1]) % SGU_CHUNK == 0 and FF_CHUNK % LANES == 0
    assert w_ffn_down.shape[1] % FF_CHUNK == 0

    f0 = 3 * D_ATT
    f1 = f0 + N_HEADS
    w_in_r = jnp.concatenate(
        [w_in[:, :, :f0], w_in[:, :, f1:], w_in[:, :, f0:f1],
         jnp.zeros((depth, d, LANES - N_HEADS), w_in.dtype)], axis=2).astype(BF16)
    b_forget_p = jnp.pad(b_forget, ((0, 0), (0, LANES - N_HEADS)))[:, None, :]
    b_gate_r = b_gate.reshape(depth, 1, N_BRANCHES * d)
    tril = jnp.tril(jnp.ones((SGU_CHUNK, SGU_CHUNK), sgu_w.dtype))
    sgu_w_cat = jnp.transpose(sgu_w * tril, (0, 2, 1, 3)).reshape(
        depth, SGU_CHUNK, N_SGU_GROUPS * SGU_CHUNK).astype(BF16)
    sgu_bias = jnp.repeat(jnp.transpose(sgu_b, (0, 2, 1)), HEAD_DIM, axis=2)
    row = lambda a: a[:, None, :]

    w_bc = w_branch_conv.astype(BF16)
    w_bs = w_branch_sgu.astype(BF16)
    w_ba = w_branch_att.astype(BF16)
    w_o = w_out.astype(BF16)
    w_up = w_ffn_up.astype(BF16)
    w_dn = w_ffn_down.astype(BF16)

    for l in range(depth):
        qa, ka, v, g0, part = _in_proj(
            l, x, row(pre_mix_g), w_in_r, b_forget_p, b_gate_r, conv_mix_w, row(sgu_ln_g), row(sgu_ln_b),
            sgu_w_cat, sgu_bias, w_bc, w_bs)
        att = _fox_attn(l, qa, ka, v)
        x = _post(l, att, g0, part, x, w_ba, w_o, row(post_mix_g), row(pre_ffn_g), w_up, conv_ffn_w, w_dn,
                  row(post_ffn_g))
    return x
```

```python
import functools

import jax
import jax.numpy as jnp
from jax import lax
from jax.experimental import pallas as pl
from jax.experimental.pallas import tpu as pltpu

F32 = jnp.float32
BF16 = jnp.bfloat16

HEAD_DIM = 64
N_HEADS = 8
D_ATT = N_HEADS * HEAD_DIM
D_CONV = 256
D_SGU = 256
N_SGU_GROUPS = 4
SGU_CHUNK = 128
CONV_WIDTH = 3
N_BRANCHES = 3
RMS_EPS = 1e-6
LN_EPS = 1e-5

LANES = 128
SUBLANES = 8
ROW_TILE = 512
ATT_TILE = 512
FF_CHUNK = 256
MASK_VALUE = -1e30
VMEM_LIMIT = 56 * 1024 * 1024

OFF_Q = 0
OFF_K = OFF_Q + D_ATT
OFF_V = OFF_K + D_ATT
OFF_BG = OFF_V + D_ATT
OFF_CG = OFF_BG + D_CONV
OFF_HC = OFF_CG + D_CONV
OFF_U = OFF_HC + D_CONV
OFF_VS = OFF_U + D_SGU
OFF_G = OFF_VS + D_SGU


def _rms(x, g):
    return x * lax.rsqrt(jnp.mean(x * x, axis=-1, keepdims=True) + RMS_EPS) * g


def _gelu_tanh(x):
    return 0.5 * x * (1.0 + jnp.tanh(0.7978845608028654 * (x + 0.044715 * (x * x * x))))


def _sigmoid(x):
    return 1.0 / (1.0 + jnp.exp(-x))


def _log_sigmoid(x):
    return jnp.minimum(x, 0.0) - jnp.log1p(jnp.exp(-jnp.abs(x)))


def _causal_conv3(z, prev, w):
    c = z.shape[1]
    rid = lax.broadcasted_iota(jnp.int32, (SUBLANES, c), 0)
    z1 = pltpu.roll(z, 1, 0)
    z2 = pltpu.roll(z, 2, 0)
    p1 = prev[SUBLANES - 1:SUBLANES, :]
    p2 = prev[SUBLANES - 2:SUBLANES - 1, :]
    top1 = jnp.where(rid == 0, p1, z1[0:SUBLANES])
    top2 = jnp.where(rid == 0, p2, jnp.where(rid == 1, p1, z2[0:SUBLANES]))
    z1 = jnp.concatenate([top1, z1[SUBLANES:]], axis=0)
    z2 = jnp.concatenate([top2, z2[SUBLANES:]], axis=0)
    return w[2:3, :] * z + w[1:2, :] * z1 + w[0:1, :] * z2


def _split3_bf16(c):
    hi = c.astype(BF16).astype(F32)
    r = c - hi
    mid = r.astype(BF16).astype(F32)
    lo = (r - mid).astype(BF16).astype(F32)
    return hi, mid, lo


def _in_proj_body(x_ref, g_ref, w_ref, bf_ref, bgate_ref, cw_ref, lng_ref, lnb_ref, sw_ref, sb_ref,
                  wbc_ref, wbs_ref, qa_ref, ka_ref, vt_ref, g0_ref, part_ref, c_carry, z_carry):
    tm = x_ref.shape[1]

    @pl.when(pl.program_id(1) == 0)
    def _():
        c_carry[...] = jnp.zeros_like(c_carry)
        z_carry[...] = jnp.zeros_like(z_carry)

    x = x_ref[0]
    xn = _rms(x, g_ref[0]).astype(BF16)

    def proj(off, width):
        return jnp.dot(xn, w_ref[0, :, off:off + width], preferred_element_type=F32)

    logf = _log_sigmoid(proj(OFF_G + N_BRANCHES * x_ref.shape[2], LANES) + bf_ref[0])
    rows = lax.broadcasted_iota(jnp.int32, (tm, LANES), 0)
    c = logf
    sh = 1
    while sh < tm:
        c = c + jnp.where(rows >= sh, pltpu.roll(c, sh, 0), 0.0)
        sh *= 2
    c = c + c_carry[SUBLANES - 1:SUBLANES, :]
    c_carry[...] = c[tm - SUBLANES:tm, :]
    c_hi, c_mid, c_lo = _split3_bf16(c)

    q = proj(OFF_Q, D_ATT) * (HEAD_DIM ** -0.5)
    k = proj(OFF_K, D_ATT)
    vt_ref[0] = proj(OFF_V, D_ATT).T.astype(BF16)
    lane = lax.broadcasted_iota(jnp.int32, (tm, LANES), 1)
    e = lane - HEAD_DIM
    k_ones = jnp.where((e >= 0) & (e < 3), 1.0, 0.0)
    q_ones = jnp.where((e >= 3) & (e < 6), 1.0, 0.0)
    for h in range(N_HEADS):
        pair = h // 2
        qp = q[:, pair * LANES:(pair + 1) * LANES]
        kp = k[:, pair * LANES:(pair + 1) * LANES]
        if h % 2 == 1:
            qp = pltpu.roll(qp, HEAD_DIM, 1)
            kp = pltpu.roll(kp, HEAD_DIM, 1)
        hi = c_hi[:, h:h + 1]
        mid = c_mid[:, h:h + 1]
        lo = c_lo[:, h:h + 1]
        q_extra = jnp.where(e == 0, hi, jnp.where(e == 1, mid, jnp.where(e == 2, lo, q_ones)))
        k_extra = jnp.where(e == 3, -hi, jnp.where(e == 4, -mid, jnp.where(e == 5, -lo, k_ones)))
        qa_ref[0, h] = jnp.where(lane < HEAD_DIM, qp, q_extra).astype(BF16)
        ka_ref[0, h] = jnp.where(lane < HEAD_DIM, kp, k_extra).astype(BF16)

    z = proj(OFF_CG, D_CONV) * proj(OFF_HC, D_CONV)
    conv_pre = proj(OFF_BG, D_CONV) * _causal_conv3(z, z_carry[...], cw_ref[0])
    z_carry[...] = z[tm - SUBLANES:tm, :]
    y_conv = jnp.dot(conv_pre.astype(BF16), wbc_ref[0], preferred_element_type=F32)

    u = _gelu_tanh(proj(OFF_U, D_SGU))
    vs = _gelu_tanh(proj(OFF_VS, D_SGU))
    mu = jnp.mean(vs, axis=-1, keepdims=True)
    vc = vs - mu
    var = jnp.mean(vc * vc, axis=-1, keepdims=True)
    vln = vc * lax.rsqrt(var + LN_EPS) * lng_ref[0] + lnb_ref[0]
    grp = lax.broadcasted_iota(jnp.int32, (SGU_CHUNK, D_SGU), 1) // HEAD_DIM
    mixed = []
    for ci in range(tm // SGU_CHUNK):
        vchunk = vln[ci * SGU_CHUNK:(ci + 1) * SGU_CHUNK, :]
        stacked = jnp.concatenate(
            [jnp.where(grp == gi, vchunk, 0.0).astype(BF16) for gi in range(N_SGU_GROUPS)], axis=0)
        mixed.append(jnp.dot(sw_ref[0], stacked, preferred_element_type=F32) + sb_ref[0])
    sgu_pre = u * jnp.concatenate(mixed, axis=0)
    y_sgu = jnp.dot(sgu_pre.astype(BF16), wbs_ref[0], preferred_element_type=F32)

    d = x_ref.shape[2]

    def gate(i):
        return _sigmoid(proj(OFF_G + i * d, d) + bgate_ref[0, :, i * d:(i + 1) * d])

    g0_ref[0] = gate(0).astype(BF16)
    part_ref[0] = (gate(1) * y_conv + gate(2) * y_sgu).astype(BF16)


def _in_proj(l, x, pre_mix_g, w_in_r, b_forget_p, b_gate_r, conv_mix_w, sgu_ln_g, sgu_ln_b, sgu_w_cat,
             sgu_bias, w_bc, w_bs):
    b, s, d = x.shape
    tm = min(ROW_TILE, s)
    win = w_in_r.shape[2]

    def layer(shape):
        nd = len(shape)
        return pl.BlockSpec((1,) + shape, lambda bi, ti: (l,) + (0,) * nd, pipeline_mode=pl.Buffered(1))

    return pl.pallas_call(
        _in_proj_body,
        grid=(b, s // tm),
        in_specs=[
            pl.BlockSpec((1, tm, d), lambda bi, ti: (bi, ti, 0)),
            layer((1, d)),
            layer((d, win)),
            layer((1, LANES)),
            layer((1, N_BRANCHES * d)),
            layer((CONV_WIDTH, D_CONV)),
            layer((1, D_SGU)),
            layer((1, D_SGU)),
            layer((SGU_CHUNK, N_SGU_GROUPS * SGU_CHUNK)),
            layer((SGU_CHUNK, D_SGU)),
            layer((D_CONV, d)),
            layer((D_SGU, d)),
        ],
        out_specs=[
            pl.BlockSpec((1, N_HEADS, tm, LANES), lambda bi, ti: (bi, 0, ti, 0)),
            pl.BlockSpec((1, N_HEADS, tm, LANES), lambda bi, ti: (bi, 0, ti, 0)),
            pl.BlockSpec((1, D_ATT, tm), lambda bi, ti: (bi, 0, ti)),
            pl.BlockSpec((1, tm, d), lambda bi, ti: (bi, ti, 0)),
            pl.BlockSpec((1, tm, d), lambda bi, ti: (bi, ti, 0)),
        ],
        out_shape=[
            jax.ShapeDtypeStruct((b, N_HEADS, s, LANES), BF16),
            jax.ShapeDtypeStruct((b, N_HEADS, s, LANES), BF16),
            jax.ShapeDtypeStruct((b, D_ATT, s), BF16),
            jax.ShapeDtypeStruct((b, s, d), BF16),
            jax.ShapeDtypeStruct((b, s, d), BF16),
        ],
        scratch_shapes=[pltpu.VMEM((SUBLANES, LANES), F32), pltpu.VMEM((SUBLANES, D_CONV), F32)],
        compiler_params=pltpu.CompilerParams(
            dimension_semantics=("arbitrary", "arbitrary"), vmem_limit_bytes=VMEM_LIMIT),
        name=f"in_proj_l{l}",
    )(x, pre_mix_g, w_in_r, b_forget_p, b_gate_r, conv_mix_w, sgu_ln_g, sgu_ln_b, sgu_w_cat, sgu_bias, w_bc, w_bs)


def _attn_body(q_ref, k_ref, vt_ref, o_ref, m_sc, l_sc, acc_sc):
    tq = q_ref.shape[2]
    qi = pl.program_id(2)
    m_sc[...] = jnp.full_like(m_sc, MASK_VALUE)
    l_sc[...] = jnp.zeros_like(l_sc)
    acc_sc[...] = jnp.zeros_like(acc_sc)

    def step(j, masked):
        kv = pl.ds(pl.multiple_of(j * tq, tq), tq)
        for hh in range(2):
            s_t = lax.dot_general(k_ref[0, hh, kv, :], q_ref[0, hh], (((1,), (1,)), ((), ())),
                                  preferred_element_type=F32)
            if masked:
                key = lax.broadcasted_iota(jnp.int32, (tq, tq), 0)
                qry = lax.broadcasted_iota(jnp.int32, (tq, tq), 1)
                s_t = jnp.where(key <= qry, s_t, MASK_VALUE)
            m_old = m_sc[hh]
            m_new = jnp.maximum(m_old, jnp.max(s_t, axis=0, keepdims=True))
            alpha = jnp.exp(m_old - m_new)
            p_t = jnp.exp(s_t - m_new)
            l_sc[hh] = alpha * l_sc[hh] + jnp.sum(p_t, axis=0, keepdims=True)
            v_t = vt_ref[0, hh * HEAD_DIM:(hh + 1) * HEAD_DIM, kv]
            acc_sc[hh] = alpha * acc_sc[hh] + jnp.dot(v_t, p_t.astype(BF16), preferred_element_type=F32)
            m_sc[hh] = m_new

    def body(j, carry):
        step(j, False)
        return carry

    lax.fori_loop(0, qi, body, 0)
    step(qi, True)
    o_t = jnp.concatenate([acc_sc[0] / l_sc[0], acc_sc[1] / l_sc[1]], axis=0)
    o_ref[0] = o_t.T.astype(BF16)


def _fox_attn(l, qa, ka, vt):
    b, _, s, _ = qa.shape
    tq = min(ATT_TILE, s)
    return pl.pallas_call(
        _attn_body,
        grid=(b, N_HEADS // 2, s // tq),
        in_specs=[
            pl.BlockSpec((1, 2, tq, LANES), lambda bi, gi, qi: (bi, gi, qi, 0)),
            pl.BlockSpec((1, 2, s, LANES), lambda bi, gi, qi: (bi, gi, 0, 0)),
            pl.BlockSpec((1, 2 * HEAD_DIM, s), lambda bi, gi, qi: (bi, gi, 0)),
        ],
        out_specs=pl.BlockSpec((1, tq, LANES), lambda bi, gi, qi: (bi, qi, gi)),
        out_shape=jax.ShapeDtypeStruct((b, s, D_ATT), BF16),
        scratch_shapes=[
            pltpu.VMEM((2, 1, tq), F32),
            pltpu.VMEM((2, 1, tq), F32),
            pltpu.VMEM((2, HEAD_DIM, tq), F32),
        ],
        compiler_params=pltpu.CompilerParams(
            dimension_semantics=("arbitrary", "arbitrary", "arbitrary"), vmem_limit_bytes=VMEM_LIMIT),
        name=f"fox_attn_l{l}",
    )(qa, ka, vt)


def _post_body(att_ref, g0_ref, part_ref, x_ref, wba_ref, wout_ref, pomg_ref, prfg_ref, wup_ref, cfw_ref,
               wdn_ref, pofg_ref, o_ref, h_carry):
    tm = x_ref.shape[1]
    d_ff = wdn_ref.shape[1]

    @pl.when(pl.program_id(1) == 0)
    def _():
        h_carry[...] = jnp.zeros_like(h_carry)

    y_att = jnp.dot(att_ref[0], wba_ref[0], preferred_element_type=F32)
    merged = g0_ref[0].astype(F32) * y_att + part_ref[0].astype(F32)
    mix = jnp.dot(merged.astype(BF16), wout_ref[0], preferred_element_type=F32)
    x1 = x_ref[0] + _rms(mix, pomg_ref[0])
    xn = _rms(x1, prfg_ref[0]).astype(BF16)

    acc = jnp.zeros((tm, x_ref.shape[2]), F32)
    for j in range(d_ff // FF_CHUNK):
        halves = []
        for off in (j * FF_CHUNK, d_ff + j * FF_CHUNK):
            h = jnp.dot(xn, wup_ref[0, :, off:off + FF_CHUNK], preferred_element_type=F32)
            halves.append(_causal_conv3(h, h_carry[:, off:off + FF_CHUNK], cfw_ref[0, :, off:off + FF_CHUNK]))
            h_carry[:, off:off + FF_CHUNK] = h[tm - SUBLANES:tm, :]
        gated = (_gelu_tanh(halves[0]) * halves[1]).astype(BF16)
        acc = acc + jnp.dot(gated, wdn_ref[0, j * FF_CHUNK:(j + 1) * FF_CHUNK, :], preferred_element_type=F32)
    o_ref[0] = x1 + _rms(acc, pofg_ref[0])


def _post(l, att, g0, part, x, w_ba, w_out, post_mix_g, pre_ffn_g, w_up, conv_ffn_w, w_down, post_ffn_g):
    b, s, d = x.shape
    tm = min(ROW_TILE, s)
    d_ff = w_down.shape[1]

    def layer(shape):
        nd = len(shape)
        return pl.BlockSpec((1,) + shape, lambda bi, ti: (l,) + (0,) * nd, pipeline_mode=pl.Buffered(1))

    def rows(width):
        return pl.BlockSpec((1, tm, width), lambda bi, ti: (bi, ti, 0))

    return pl.pallas_call(
        _post_body,
        grid=(b, s // tm),
        in_specs=[
            rows(D_ATT), rows(d), rows(d), rows(d),
            layer((D_ATT, d)),
            layer((d, d)),
            layer((1, d)),
            layer((1, d)),
            layer((d, 2 * d_ff)),
            layer((CONV_WIDTH, 2 * d_ff)),
            layer((d_ff, d)),
            layer((1, d)),
        ],
        out_specs=rows(d),
        out_shape=jax.ShapeDtypeStruct((b, s, d), F32),
        scratch_shapes=[pltpu.VMEM((SUBLANES, 2 * d_ff), F32)],
        compiler_params=pltpu.CompilerParams(
            dimension_semantics=("arbitrary", "arbitrary"), vmem_limit_bytes=VMEM_LIMIT),
        name=f"post_l{l}",
    )(att, g0, part, x, w_ba, w_out, post_mix_g, pre_ffn_g, w_up, conv_ffn_w, w_down, post_ffn_g)


def kernel(x, pre_mix_g, post_mix_g, pre_ffn_g, post_ffn_g, w_in, b_forget, b_gate, conv_mix_w, sgu_ln_g,
           sgu_ln_b, sgu_w, sgu_b, w_branch_att, w_branch_conv, w_branch_sgu, w_out, w_ffn_up, conv_ffn_w,
           w_ffn_down):
    depth, d, _ = w_in.shape
    assert d % LANES == 0 and x.shape[1] % min(ROW_TILE, x.shape[1]) == 0
    assert min(ROW_TILE, x.shape[1]) % SGU_CHUNK == 0 and FF_CHUNK % LANES == 0
    assert w_ffn_down.shape[1] % FF_CHUNK == 0

    f0 = 3 * D_ATT
    f1 = f0 + N_HEADS
    w_in_r = jnp.concatenate(
        [w_in[:, :, :f0], w_in[:, :, f1:], w_in[:, :, f0:f1],
         jnp.zeros((depth, d, LANES - N_HEADS), w_in.dtype)], axis=2).astype(BF16)
    b_forget_p = jnp.pad(b_forget, ((0, 0), (0, LANES - N_HEADS)))[:, None, :]
    b_gate_r = b_gate.reshape(depth, 1, N_BRANCHES * d)
    tril = jnp.tril(jnp.ones((SGU_CHUNK, SGU_CHUNK), sgu_w.dtype))
    sgu_w_cat = jnp.transpose(sgu_w * tril, (0, 2, 1, 3)).reshape(
        depth, SGU_CHUNK, N_SGU_GROUPS * SGU_CHUNK).astype(BF16)
    sgu_bias = jnp.repeat(jnp.transpose(sgu_b, (0, 2, 1)), HEAD_DIM, axis=2)
    row = lambda a: a[:, None, :]

    w_bc = w_branch_conv.astype(BF16)
    w_bs = w_branch_sgu.astype(BF16)
    w_ba = w_branch_att.astype(BF16)
    w_o = w_out.astype(BF16)
    w_up = w_ffn_up.astype(BF16)
    w_dn = w_ffn_down.astype(BF16)

    for l in range(depth):
        qa, ka, vt, g0, part = _in_proj(
            l, x, row(pre_mix_g), w_in_r, b_forget_p, b_gate_r, conv_mix_w, row(sgu_ln_g), row(sgu_ln_b),
            sgu_w_cat, sgu_bias, w_bc, w_bs)
        att = _fox_attn(l, qa, ka, vt)
        x = _post(l, att, g0, part, x, w_ba, w_o, row(post_mix_g), row(pre_ffn_g), w_up, conv_ffn_w, w_dn,
                  row(post_ffn_g))
    return x
```

```python
import functools

import jax
import jax.numpy as jnp
from jax import lax
from jax.experimental import pallas as pl
from jax.experimental.pallas import tpu as pltpu

F32 = jnp.float32
BF16 = jnp.bfloat16

HEAD_DIM = 64
N_HEADS = 8
D_ATT = N_HEADS * HEAD_DIM
D_CONV = 256
D_SGU = 256
N_SGU_GROUPS = 4
SGU_CHUNK = 128
CONV_WIDTH = 3
N_BRANCHES = 3
RMS_EPS = 1e-6
LN_EPS = 1e-5

LANES = 128
SUBLANES = 8
ROW_TILE = 512
ATT_TILE = 512
FF_CHUNK = 256
MASK_VALUE = -1e30
VMEM_LIMIT = 56 * 1024 * 1024

OFF_Q = 0
OFF_K = OFF_Q + D_ATT
OFF_V = OFF_K + D_ATT
OFF_BG = OFF_V + D_ATT
OFF_CG = OFF_BG + D_CONV
OFF_HC = OFF_CG + D_CONV
OFF_U = OFF_HC + D_CONV
OFF_VS = OFF_U + D_SGU
OFF_G = OFF_VS + D_SGU


def _rms(x, g):
    return x * lax.rsqrt(jnp.mean(x * x, axis=-1, keepdims=True) + RMS_EPS) * g


def _gelu_tanh(x):
    return 0.5 * x * (1.0 + jnp.tanh(0.7978845608028654 * (x + 0.044715 * (x * x * x))))


def _sigmoid(x):
    return 1.0 / (1.0 + jnp.exp(-x))


def _log_sigmoid(x):
    return jnp.minimum(x, 0.0) - jnp.log1p(jnp.exp(-jnp.abs(x)))


def _stage_rows(buf, lead, slab, h, carry, cols):
    tm = h.shape[0]
    buf[lead, slab, 0:SUBLANES, :] = carry[:, cols]
    buf[lead, slab, SUBLANES:, :] = h
    carry[:, cols] = h[tm - SUBLANES:tm, :]


def _staged_rows(buf, lead, slab, shift, tm):
    if shift == 0:
        return buf[lead, slab, SUBLANES:tm + SUBLANES, :]
    return buf[pl.ds(lead, 1, stride=2), slab, pl.ds(SUBLANES - shift, tm), :][0]


def _staged_conv3(buf, lead, slab, tm, w):
    return (w[2:3, :] * _staged_rows(buf, lead, slab, 0, tm)
            + w[1:2, :] * _staged_rows(buf, lead, slab, 1, tm)
            + w[0:1, :] * _staged_rows(buf, lead, slab, 2, tm))


def _split3_bf16(c):
    hi = c.astype(BF16).astype(F32)
    r = c - hi
    mid = r.astype(BF16).astype(F32)
    lo = (r - mid).astype(BF16).astype(F32)
    return hi, mid, lo


def _in_proj_body(x_ref, g_ref, w_ref, bf_ref, bgate_ref, cw_ref, lng_ref, lnb_ref, sw_ref, sb_ref,
                  wbc_ref, wbs_ref, qa_ref, ka_ref, vt_ref, g0_ref, part_ref, c_carry, z_carry, zbuf):
    tm = x_ref.shape[1]

    @pl.when(pl.program_id(1) == 0)
    def _():
        c_carry[...] = jnp.zeros_like(c_carry)
        z_carry[...] = jnp.zeros_like(z_carry)

    x = x_ref[0]
    xn = _rms(x, g_ref[0]).astype(BF16)

    def proj(off, width):
        return jnp.dot(xn, w_ref[0, :, off:off + width], preferred_element_type=F32)

    logf = _log_sigmoid(proj(OFF_G + N_BRANCHES * x_ref.shape[2], LANES) + bf_ref[0])
    rows = lax.broadcasted_iota(jnp.int32, (tm, LANES), 0)
    c = logf
    sh = 1
    while sh < tm:
        c = c + jnp.where(rows >= sh, pltpu.roll(c, sh, 0), 0.0)
        sh *= 2
    c = c + c_carry[SUBLANES - 1:SUBLANES, :]
    c_carry[...] = c[tm - SUBLANES:tm, :]
    c_hi, c_mid, c_lo = _split3_bf16(c)

    q = proj(OFF_Q, D_ATT) * (HEAD_DIM ** -0.5)
    k = proj(OFF_K, D_ATT)
    vt_ref[0] = proj(OFF_V, D_ATT).T.astype(BF16)
    lane = lax.broadcasted_iota(jnp.int32, (tm, LANES), 1)
    e = lane - HEAD_DIM
    k_ones = jnp.where((e >= 0) & (e < 3), 1.0, 0.0)
    q_ones = jnp.where((e >= 3) & (e < 6), 1.0, 0.0)
    for h in range(N_HEADS):
        pair = h // 2
        qp = q[:, pair * LANES:(pair + 1) * LANES]
        kp = k[:, pair * LANES:(pair + 1) * LANES]
        if h % 2 == 1:
            qp = pltpu.roll(qp, HEAD_DIM, 1)
            kp = pltpu.roll(kp, HEAD_DIM, 1)
        hi = c_hi[:, h:h + 1]
        mid = c_mid[:, h:h + 1]
        lo = c_lo[:, h:h + 1]
        q_extra = jnp.where(e == 0, hi, jnp.where(e == 1, mid, jnp.where(e == 2, lo, q_ones)))
        k_extra = jnp.where(e == 3, -hi, jnp.where(e == 4, -mid, jnp.where(e == 5, -lo, k_ones)))
        qa_ref[0, h] = jnp.where(lane < HEAD_DIM, qp, q_extra).astype(BF16)
        ka_ref[0, h] = jnp.where(lane < HEAD_DIM, kp, k_extra).astype(BF16)

    z = proj(OFF_CG, D_CONV) * proj(OFF_HC, D_CONV)
    conv_slabs = []
    for s in range(D_CONV // LANES):
        cols = slice(s * LANES, (s + 1) * LANES)
        _stage_rows(zbuf, 0, s, z[:, cols], z_carry, cols)
        conv_slabs.append(_staged_conv3(zbuf, 0, s, tm, cw_ref[0, :, cols]))
    conv_pre = proj(OFF_BG, D_CONV) * jnp.concatenate(conv_slabs, axis=1)
    y_conv = jnp.dot(conv_pre.astype(BF16), wbc_ref[0], preferred_element_type=F32)

    u = _gelu_tanh(proj(OFF_U, D_SGU))
    vs = _gelu_tanh(proj(OFF_VS, D_SGU))
    mu = jnp.mean(vs, axis=-1, keepdims=True)
    vc = vs - mu
    var = jnp.mean(vc * vc, axis=-1, keepdims=True)
    vln = vc * lax.rsqrt(var + LN_EPS) * lng_ref[0] + lnb_ref[0]
    grp = lax.broadcasted_iota(jnp.int32, (SGU_CHUNK, D_SGU), 1) // HEAD_DIM
    mixed = []
    for ci in range(tm // SGU_CHUNK):
        vchunk = vln[ci * SGU_CHUNK:(ci + 1) * SGU_CHUNK, :]
        stacked = jnp.concatenate(
            [jnp.where(grp == gi, vchunk, 0.0).astype(BF16) for gi in range(N_SGU_GROUPS)], axis=0)
        mixed.append(jnp.dot(sw_ref[0], stacked, preferred_element_type=F32) + sb_ref[0])
    sgu_pre = u * jnp.concatenate(mixed, axis=0)
    y_sgu = jnp.dot(sgu_pre.astype(BF16), wbs_ref[0], preferred_element_type=F32)

    d = x_ref.shape[2]

    def gate(i):
        return _sigmoid(proj(OFF_G + i * d, d) + bgate_ref[0, :, i * d:(i + 1) * d])

    g0_ref[0] = gate(0).astype(BF16)
    part_ref[0] = (gate(1) * y_conv + gate(2) * y_sgu).astype(BF16)


def _in_proj(l, x, pre_mix_g, w_in_r, b_forget_p, b_gate_r, conv_mix_w, sgu_ln_g, sgu_ln_b, sgu_w_cat,
             sgu_bias, w_bc, w_bs):
    b, s, d = x.shape
    tm = min(ROW_TILE, s)
    win = w_in_r.shape[2]

    def layer(shape):
        nd = len(shape)
        return pl.BlockSpec((1,) + shape, lambda bi, ti: (l,) + (0,) * nd, pipeline_mode=pl.Buffered(1))

    return pl.pallas_call(
        _in_proj_body,
        grid=(b, s // tm),
        in_specs=[
            pl.BlockSpec((1, tm, d), lambda bi, ti: (bi, ti, 0)),
            layer((1, d)),
            layer((d, win)),
            layer((1, LANES)),
            layer((1, N_BRANCHES * d)),
            layer((CONV_WIDTH, D_CONV)),
            layer((1, D_SGU)),
            layer((1, D_SGU)),
            layer((SGU_CHUNK, N_SGU_GROUPS * SGU_CHUNK)),
            layer((SGU_CHUNK, D_SGU)),
            layer((D_CONV, d)),
            layer((D_SGU, d)),
        ],
        out_specs=[
            pl.BlockSpec((1, N_HEADS, tm, LANES), lambda bi, ti: (bi, 0, ti, 0)),
            pl.BlockSpec((1, N_HEADS, tm, LANES), lambda bi, ti: (bi, 0, ti, 0)),
            pl.BlockSpec((1, D_ATT, tm), lambda bi, ti: (bi, 0, ti)),
            pl.BlockSpec((1, tm, d), lambda bi, ti: (bi, ti, 0)),
            pl.BlockSpec((1, tm, d), lambda bi, ti: (bi, ti, 0)),
        ],
        out_shape=[
            jax.ShapeDtypeStruct((b, N_HEADS, s, LANES), BF16),
            jax.ShapeDtypeStruct((b, N_HEADS, s, LANES), BF16),
            jax.ShapeDtypeStruct((b, D_ATT, s), BF16),
            jax.ShapeDtypeStruct((b, s, d), BF16),
            jax.ShapeDtypeStruct((b, s, d), BF16),
        ],
        scratch_shapes=[pltpu.VMEM((SUBLANES, LANES), F32), pltpu.VMEM((SUBLANES, D_CONV), F32),
                        pltpu.VMEM((1, D_CONV // LANES, tm + SUBLANES, LANES), F32)],
        compiler_params=pltpu.CompilerParams(
            dimension_semantics=("arbitrary", "arbitrary"), vmem_limit_bytes=VMEM_LIMIT),
        name=f"in_proj_l{l}",
    )(x, pre_mix_g, w_in_r, b_forget_p, b_gate_r, conv_mix_w, sgu_ln_g, sgu_ln_b, sgu_w_cat, sgu_bias, w_bc, w_bs)


def _attn_body(q_ref, k_ref, vt_ref, o_ref, m_sc, l_sc, acc_sc):
    tq = q_ref.shape[2]
    qi = pl.program_id(2)
    m_sc[...] = jnp.full_like(m_sc, MASK_VALUE)
    l_sc[...] = jnp.zeros_like(l_sc)
    acc_sc[...] = jnp.zeros_like(acc_sc)

    def step(j, masked):
        kv = pl.ds(pl.multiple_of(j * tq, tq), tq)
        for hh in range(2):
            s_t = lax.dot_general(k_ref[0, hh, kv, :], q_ref[0, hh], (((1,), (1,)), ((), ())),
                                  preferred_element_type=F32)
            if masked:
                key = lax.broadcasted_iota(jnp.int32, (tq, tq), 0)
                qry = lax.broadcasted_iota(jnp.int32, (tq, tq), 1)
                s_t = jnp.where(key <= qry, s_t, MASK_VALUE)
            m_old = m_sc[hh]
            m_new = jnp.maximum(m_old, jnp.max(s_t, axis=0, keepdims=True))
            alpha = jnp.exp(m_old - m_new)
            p_t = jnp.exp(s_t - m_new)
            l_sc[hh] = alpha * l_sc[hh] + jnp.sum(p_t, axis=0, keepdims=True)
            v_t = vt_ref[0, hh * HEAD_DIM:(hh + 1) * HEAD_DIM, kv]
            acc_sc[hh] = alpha * acc_sc[hh] + jnp.dot(v_t, p_t.astype(BF16), preferred_element_type=F32)
            m_sc[hh] = m_new

    def body(j, carry):
        step(j, False)
        return carry

    lax.fori_loop(0, qi, body, 0)
    step(qi, True)
    o_t = jnp.concatenate([acc_sc[0] / l_sc[0], acc_sc[1] / l_sc[1]], axis=0)
    o_ref[0] = o_t.T.astype(BF16)


def _fox_attn(l, qa, ka, vt):
    b, _, s, _ = qa.shape
    tq = min(ATT_TILE, s)
    return pl.pallas_call(
        _attn_body,
        grid=(b, N_HEADS // 2, s // tq),
        in_specs=[
            pl.BlockSpec((1, 2, tq, LANES), lambda bi, gi, qi: (bi, gi, qi, 0)),
            pl.BlockSpec((1, 2, s, LANES), lambda bi, gi, qi: (bi, gi, 0, 0)),
            pl.BlockSpec((1, 2 * HEAD_DIM, s), lambda bi, gi, qi: (bi, gi, 0)),
        ],
        out_specs=pl.BlockSpec((1, tq, LANES), lambda bi, gi, qi: (bi, qi, gi)),
        out_shape=jax.ShapeDtypeStruct((b, s, D_ATT), BF16),
        scratch_shapes=[
            pltpu.VMEM((2, 1, tq), F32),
            pltpu.VMEM((2, 1, tq), F32),
            pltpu.VMEM((2, HEAD_DIM, tq), F32),
        ],
        compiler_params=pltpu.CompilerParams(
            dimension_semantics=("arbitrary", "arbitrary", "arbitrary"), vmem_limit_bytes=VMEM_LIMIT),
        name=f"fox_attn_l{l}",
    )(qa, ka, vt)


def _post_body(att_ref, g0_ref, part_ref, x_ref, wba_ref, wout_ref, pomg_ref, prfg_ref, wup_ref, cfw_ref,
               wdn_ref, pofg_ref, o_ref, h_carry, hbuf):
    tm = x_ref.shape[1]
    d_ff = wdn_ref.shape[1]
    n_chunks = d_ff // FF_CHUNK

    @pl.when(pl.program_id(1) == 0)
    def _():
        h_carry[...] = jnp.zeros_like(h_carry)

    y_att = jnp.dot(att_ref[0], wba_ref[0], preferred_element_type=F32)
    merged = g0_ref[0].astype(F32) * y_att + part_ref[0].astype(F32)
    mix = jnp.dot(merged.astype(BF16), wout_ref[0], preferred_element_type=F32)
    x1 = x_ref[0] + _rms(mix, pomg_ref[0])
    xn = _rms(x1, prfg_ref[0]).astype(BF16)

    def offsets(j):
        return (j * FF_CHUNK, d_ff + j * FF_CHUNK)

    slabs = FF_CHUNK // LANES

    def up_project(j):
        for half, off in enumerate(offsets(j)):
            h = jnp.dot(xn, wup_ref[0, :, off:off + FF_CHUNK], preferred_element_type=F32)
            for s in range(slabs):
                cols = slice(off + s * LANES, off + (s + 1) * LANES)
                _stage_rows(hbuf, j % 2, half * slabs + s, h[:, s * LANES:(s + 1) * LANES], h_carry, cols)

    def conv(j, half, off):
        return jnp.concatenate(
            [_staged_conv3(hbuf, j % 2, half * slabs + s, tm,
                           cfw_ref[0, :, off + s * LANES:off + (s + 1) * LANES]) for s in range(slabs)], axis=1)

    def down_project(j, gated):
        return jnp.dot(gated, wdn_ref[0, j * FF_CHUNK:(j + 1) * FF_CHUNK, :], preferred_element_type=F32)

    up_project(0)
    acc = jnp.zeros((tm, x_ref.shape[2]), F32)
    gated_prev = None
    for j in range(n_chunks):
        if j + 1 < n_chunks:
            up_project(j + 1)
        if gated_prev is not None:
            acc = acc + down_project(j - 1, gated_prev)
        off_a, off_b = offsets(j)
        gated_prev = (_gelu_tanh(conv(j, 0, off_a)) * conv(j, 1, off_b)).astype(BF16)
    acc = acc + down_project(n_chunks - 1, gated_prev)
    o_ref[0] = x1 + _rms(acc, pofg_ref[0])


def _post(l, att, g0, part, x, w_ba, w_out, post_mix_g, pre_ffn_g, w_up, conv_ffn_w, w_down, post_ffn_g):
    b, s, d = x.shape
    tm = min(ROW_TILE, s)
    d_ff = w_down.shape[1]

    def layer(shape):
        nd = len(shape)
        return pl.BlockSpec((1,) + shape, lambda bi, ti: (l,) + (0,) * nd, pipeline_mode=pl.Buffered(1))

    def rows(width):
        return pl.BlockSpec((1, tm, width), lambda bi, ti: (bi, ti, 0))

    return pl.pallas_call(
        _post_body,
        grid=(b, s // tm),
        in_specs=[
            rows(D_ATT), rows(d), rows(d), rows(d),
            layer((D_ATT, d)),
            layer((d, d)),
            layer((1, d)),
            layer((1, d)),
            layer((d, 2 * d_ff)),
            layer((CONV_WIDTH, 2 * d_ff)),
            layer((d_ff, d)),
            layer((1, d)),
        ],
        out_specs=rows(d),
        out_shape=jax.ShapeDtypeStruct((b, s, d), F32),
        scratch_shapes=[pltpu.VMEM((SUBLANES, 2 * d_ff), F32),
                        pltpu.VMEM((2, 2 * FF_CHUNK // LANES, tm + SUBLANES, LANES), F32)],
        compiler_params=pltpu.CompilerParams(
            dimension_semantics=("arbitrary", "arbitrary"), vmem_limit_bytes=VMEM_LIMIT),
        name=f"post_l{l}",
    )(att, g0, part, x, w_ba, w_out, post_mix_g, pre_ffn_g, w_up, conv_ffn_w, w_down, post_ffn_g)


def kernel(x, pre_mix_g, post_mix_g, pre_ffn_g, post_ffn_g, w_in, b_forget, b_gate, conv_mix_w, sgu_ln_g,
           sgu_ln_b, sgu_w, sgu_b, w_branch_att, w_branch_conv, w_branch_sgu, w_out, w_ffn_up, conv_ffn_w,
           w_ffn_down):
    depth, d, _ = w_in.shape
    assert d % LANES == 0 and x.shape[1] % min(ROW_TILE, x.shape[1]) == 0
    assert min(ROW_TILE, x.shape[1]) % SGU_CHUNK == 0 and FF_CHUNK % LANES == 0
    assert w_ffn_down.shape[1] % FF_CHUNK == 0

    f0 = 3 * D_ATT
    f1 = f0 + N_HEADS
    w_in_r = jnp.concatenate(
        [w_in[:, :, :f0], w_in[:, :, f1:], w_in[:, :, f0:f1],
         jnp.zeros((depth, d, LANES - N_HEADS), w_in.dtype)], axis=2).astype(BF16)
    b_forget_p = jnp.pad(b_forget, ((0, 0), (0, LANES - N_HEADS)))[:, None, :]
    b_gate_r = b_gate.reshape(depth, 1, N_BRANCHES * d)
    tril = jnp.tril(jnp.ones((SGU_CHUNK, SGU_CHUNK), sgu_w.dtype))
    sgu_w_cat = jnp.transpose(sgu_w * tril, (0, 2, 1, 3)).reshape(
        depth, SGU_CHUNK, N_SGU_GROUPS * SGU_CHUNK).astype(BF16)
    sgu_bias = jnp.repeat(jnp.transpose(sgu_b, (0, 2, 1)), HEAD_DIM, axis=2)
    row = lambda a: a[:, None, :]

    w_bc = w_branch_conv.astype(BF16)
    w_bs = w_branch_sgu.astype(BF16)
    w_ba = w_branch_att.astype(BF16)
    w_o = w_out.astype(BF16)
    w_up = w_ffn_up.astype(BF16)
    w_dn = w_ffn_down.astype(BF16)

    for l in range(depth):
        qa, ka, vt, g0, part = _in_proj(
            l, x, row(pre_mix_g), w_in_r, b_forget_p, b_gate_r, conv_mix_w, row(sgu_ln_g), row(sgu_ln_b),
            sgu_w_cat, sgu_bias, w_bc, w_bs)
        att = _fox_attn(l, qa, ka, vt)
        x = _post(l, att, g0, part, x, w_ba, w_o, row(post_mix_g), row(pre_ffn_g), w_up, conv_ffn_w, w_dn,
                  row(post_ffn_g))
    return x
```

```python
import functools

import jax
import jax.numpy as jnp
from jax import lax
from jax.experimental import pallas as pl
from jax.experimental.pallas import tpu as pltpu

F32 = jnp.float32
BF16 = jnp.bfloat16

HEAD_DIM = 64
N_HEADS = 8
D_ATT = N_HEADS * HEAD_DIM
D_CONV = 256
D_SGU = 256
N_SGU_GROUPS = 4
SGU_CHUNK = 128
CONV_WIDTH = 3
N_BRANCHES = 3
RMS_EPS = 1e-6
LN_EPS = 1e-5
LOG2E = 1.4426950408889634

LANES = 128
SUBLANES = 8
ROW_TILE = 512
ATT_TILE = 512
FF_CHUNK = 256
MASK_VALUE = -1e30
VMEM_LIMIT = 56 * 1024 * 1024

OFF_Q = 0
OFF_K = OFF_Q + D_ATT
OFF_V = OFF_K + D_ATT
OFF_BG = OFF_V + D_ATT
OFF_CG = OFF_BG + D_CONV
OFF_HC = OFF_CG + D_CONV
OFF_U = OFF_HC + D_CONV
OFF_VS = OFF_U + D_SGU
OFF_G = OFF_VS + D_SGU


def _rms(x, g):
    return x * lax.rsqrt(jnp.mean(x * x, axis=-1, keepdims=True) + RMS_EPS) * g


def _gelu_tanh(x):
    return 0.5 * x * (1.0 + jnp.tanh(0.7978845608028654 * (x + 0.044715 * (x * x * x))))


def _sigmoid(x):
    return 1.0 / (1.0 + jnp.exp(-x))


def _log_sigmoid(x):
    return jnp.minimum(x, 0.0) - jnp.log1p(jnp.exp(-jnp.abs(x)))


def _stage_rows(buf, lead, slab, h, carry, cols):
    tm = h.shape[0]
    buf[lead, slab, 0:SUBLANES, :] = carry[:, cols]
    buf[lead, slab, SUBLANES:, :] = h
    carry[:, cols] = h[tm - SUBLANES:tm, :]


def _staged_rows(buf, lead, slab, shift, tm):
    if shift == 0:
        return buf[lead, slab, SUBLANES:tm + SUBLANES, :]
    return buf[pl.ds(lead, 1, stride=2), slab, pl.ds(SUBLANES - shift, tm), :][0]


def _staged_conv3(buf, lead, slab, tm, w):
    return (w[2:3, :] * _staged_rows(buf, lead, slab, 0, tm)
            + w[1:2, :] * _staged_rows(buf, lead, slab, 1, tm)
            + w[0:1, :] * _staged_rows(buf, lead, slab, 2, tm))


def _split3_bf16(c):
    hi = c.astype(BF16).astype(F32)
    r = c - hi
    mid = r.astype(BF16).astype(F32)
    lo = (r - mid).astype(BF16).astype(F32)
    return hi, mid, lo


def _in_proj_body(x_ref, g_ref, w_ref, bf_ref, bgate_ref, cw_ref, lng_ref, lnb_ref, sw_ref, sb_ref,
                  wbc_ref, wbs_ref, qa_ref, ka_ref, vt_ref, g0_ref, part_ref, c_carry, z_carry, zbuf):
    tm = x_ref.shape[1]

    @pl.when(pl.program_id(1) == 0)
    def _():
        c_carry[...] = jnp.zeros_like(c_carry)
        z_carry[...] = jnp.zeros_like(z_carry)

    x = x_ref[0]
    xn = _rms(x, g_ref[0]).astype(BF16)

    def proj(off, width):
        return jnp.dot(xn, w_ref[0, :, off:off + width], preferred_element_type=F32)

    logf = _log_sigmoid(proj(OFF_G + N_BRANCHES * x_ref.shape[2], LANES) + bf_ref[0])
    rows = lax.broadcasted_iota(jnp.int32, (tm, LANES), 0)
    c = logf
    sh = 1
    while sh < tm:
        c = c + jnp.where(rows >= sh, pltpu.roll(c, sh, 0), 0.0)
        sh *= 2
    c = c + c_carry[SUBLANES - 1:SUBLANES, :]
    c_carry[...] = c[tm - SUBLANES:tm, :]
    c_hi, c_mid, c_lo = _split3_bf16(c * LOG2E)

    q = proj(OFF_Q, D_ATT) * (HEAD_DIM ** -0.5 * LOG2E)
    k = proj(OFF_K, D_ATT)
    vt_ref[0] = proj(OFF_V, D_ATT).T.astype(BF16)
    lane = lax.broadcasted_iota(jnp.int32, (tm, LANES), 1)
    e = lane - HEAD_DIM
    k_ones = jnp.where((e >= 0) & (e < 3), 1.0, 0.0)
    q_ones = jnp.where((e >= 3) & (e < 6), 1.0, 0.0)
    for h in range(N_HEADS):
        pair = h // 2
        qp = q[:, pair * LANES:(pair + 1) * LANES]
        kp = k[:, pair * LANES:(pair + 1) * LANES]
        if h % 2 == 1:
            qp = pltpu.roll(qp, HEAD_DIM, 1)
            kp = pltpu.roll(kp, HEAD_DIM, 1)
        hi = c_hi[:, h:h + 1]
        mid = c_mid[:, h:h + 1]
        lo = c_lo[:, h:h + 1]
        q_extra = jnp.where(e == 0, hi, jnp.where(e == 1, mid, jnp.where(e == 2, lo, q_ones)))
        k_extra = jnp.where(e == 3, -hi, jnp.where(e == 4, -mid, jnp.where(e == 5, -lo, k_ones)))
        qa_ref[0, h] = jnp.where(lane < HEAD_DIM, qp, q_extra).astype(BF16)
        ka_ref[0, h] = jnp.where(lane < HEAD_DIM, kp, k_extra).astype(BF16)

    z = proj(OFF_CG, D_CONV) * proj(OFF_HC, D_CONV)
    conv_slabs = []
    for s in range(D_CONV // LANES):
        cols = slice(s * LANES, (s + 1) * LANES)
        _stage_rows(zbuf, 0, s, z[:, cols], z_carry, cols)
        conv_slabs.append(_staged_conv3(zbuf, 0, s, tm, cw_ref[0, :, cols]))
    conv_pre = proj(OFF_BG, D_CONV) * jnp.concatenate(conv_slabs, axis=1)
    y_conv = jnp.dot(conv_pre.astype(BF16), wbc_ref[0], preferred_element_type=F32)

    u = _gelu_tanh(proj(OFF_U, D_SGU))
    vs = _gelu_tanh(proj(OFF_VS, D_SGU))
    mu = jnp.mean(vs, axis=-1, keepdims=True)
    vc = vs - mu
    var = jnp.mean(vc * vc, axis=-1, keepdims=True)
    vln = vc * lax.rsqrt(var + LN_EPS) * lng_ref[0] + lnb_ref[0]
    grp = lax.broadcasted_iota(jnp.int32, (SGU_CHUNK, D_SGU), 1) // HEAD_DIM
    mixed = []
    for ci in range(tm // SGU_CHUNK):
        vchunk = vln[ci * SGU_CHUNK:(ci + 1) * SGU_CHUNK, :]
        stacked = jnp.concatenate(
            [jnp.where(grp == gi, vchunk, 0.0).astype(BF16) for gi in range(N_SGU_GROUPS)], axis=0)
        mixed.append(jnp.dot(sw_ref[0], stacked, preferred_element_type=F32) + sb_ref[0])
    sgu_pre = u * jnp.concatenate(mixed, axis=0)
    y_sgu = jnp.dot(sgu_pre.astype(BF16), wbs_ref[0], preferred_element_type=F32)

    d = x_ref.shape[2]

    def gate(i):
        return _sigmoid(proj(OFF_G + i * d, d) + bgate_ref[0, :, i * d:(i + 1) * d])

    g0_ref[0] = gate(0).astype(BF16)
    part_ref[0] = (gate(1) * y_conv + gate(2) * y_sgu).astype(BF16)


def _in_proj(l, x, pre_mix_g, w_in_r, b_forget_p, b_gate_r, conv_mix_w, sgu_ln_g, sgu_ln_b, sgu_w_cat,
             sgu_bias, w_bc, w_bs):
    b, s, d = x.shape
    tm = min(ROW_TILE, s)
    win = w_in_r.shape[2]

    def layer(shape):
        nd = len(shape)
        return pl.BlockSpec((1,) + shape, lambda bi, ti: (l,) + (0,) * nd, pipeline_mode=pl.Buffered(1))

    return pl.pallas_call(
        _in_proj_body,
        grid=(b, s // tm),
        in_specs=[
            pl.BlockSpec((1, tm, d), lambda bi, ti: (bi, ti, 0)),
            layer((1, d)),
            layer((d, win)),
            layer((1, LANES)),
            layer((1, N_BRANCHES * d)),
            layer((CONV_WIDTH, D_CONV)),
            layer((1, D_SGU)),
            layer((1, D_SGU)),
            layer((SGU_CHUNK, N_SGU_GROUPS * SGU_CHUNK)),
            layer((SGU_CHUNK, D_SGU)),
            layer((D_CONV, d)),
            layer((D_SGU, d)),
        ],
        out_specs=[
            pl.BlockSpec((1, N_HEADS, tm, LANES), lambda bi, ti: (bi, 0, ti, 0)),
            pl.BlockSpec((1, N_HEADS, tm, LANES), lambda bi, ti: (bi, 0, ti, 0)),
            pl.BlockSpec((1, D_ATT, tm), lambda bi, ti: (bi, 0, ti)),
            pl.BlockSpec((1, tm, d), lambda bi, ti: (bi, ti, 0)),
            pl.BlockSpec((1, tm, d), lambda bi, ti: (bi, ti, 0)),
        ],
        out_shape=[
            jax.ShapeDtypeStruct((b, N_HEADS, s, LANES), BF16),
            jax.ShapeDtypeStruct((b, N_HEADS, s, LANES), BF16),
            jax.ShapeDtypeStruct((b, D_ATT, s), BF16),
            jax.ShapeDtypeStruct((b, s, d), BF16),
            jax.ShapeDtypeStruct((b, s, d), BF16),
        ],
        scratch_shapes=[pltpu.VMEM((SUBLANES, LANES), F32), pltpu.VMEM((SUBLANES, D_CONV), F32),
                        pltpu.VMEM((1, D_CONV // LANES, tm + SUBLANES, LANES), F32)],
        compiler_params=pltpu.CompilerParams(
            dimension_semantics=("arbitrary", "arbitrary"), vmem_limit_bytes=VMEM_LIMIT),
        name=f"in_proj_l{l}",
    )(x, pre_mix_g, w_in_r, b_forget_p, b_gate_r, conv_mix_w, sgu_ln_g, sgu_ln_b, sgu_w_cat, sgu_bias, w_bc, w_bs)


def _attn_body(q_ref, k_ref, vt_ref, o_ref, s_buf, m_sc, l_sc, acc_sc):
    tq = q_ref.shape[2]
    qi = pl.program_id(2)
    m_sc[...] = jnp.full_like(m_sc, MASK_VALUE)
    l_sc[...] = jnp.zeros_like(l_sc)
    acc_sc[...] = jnp.zeros_like(acc_sc)

    def scores(j, slot):
        kv = pl.ds(pl.multiple_of(j * tq, tq), tq)
        for hh in range(2):
            s_buf[slot, hh] = lax.dot_general(k_ref[0, hh, kv, :], q_ref[0, hh], (((1,), (1,)), ((), ())),
                                              preferred_element_type=F32)

    def update(j, slot, masked):
        kv = pl.ds(pl.multiple_of(j * tq, tq), tq)
        for hh in range(2):
            s_t = s_buf[slot, hh]
            if masked:
                key = lax.broadcasted_iota(jnp.int32, (tq, tq), 0)
                qry = lax.broadcasted_iota(jnp.int32, (tq, tq), 1)
                s_t = jnp.where(key <= qry, s_t, MASK_VALUE)
            m_old = m_sc[hh]
            m_new = jnp.maximum(m_old, jnp.max(s_t, axis=0, keepdims=True))
            alpha = jnp.exp2(m_old - m_new)
            p_t = jnp.exp2(s_t - m_new)
            l_sc[hh] = alpha * l_sc[hh] + jnp.sum(p_t, axis=0, keepdims=True)
            v_t = vt_ref[0, hh * HEAD_DIM:(hh + 1) * HEAD_DIM, kv]
            acc_sc[hh] = alpha * acc_sc[hh] + jnp.dot(v_t, p_t.astype(BF16), preferred_element_type=F32)
            m_sc[hh] = m_new

    scores(0, 0)

    def pair(i, carry):
        j = 2 * i
        scores(j + 1, 1)
        update(j, 0, False)
        scores(j + 2, 0)
        update(j + 1, 1, False)
        return carry

    lax.fori_loop(0, lax.shift_right_logical(qi, 1), pair, 0)
    odd = lax.bitwise_and(qi, 1) == 1

    @pl.when(odd)
    def _():
        scores(qi, 1)
        update(qi - 1, 0, False)
        update(qi, 1, True)

    @pl.when(jnp.logical_not(odd))
    def _():
        update(qi, 0, True)
    o_t = jnp.concatenate([acc_sc[0] / l_sc[0], acc_sc[1] / l_sc[1]], axis=0)
    o_ref[0] = o_t.T.astype(BF16)


def _fox_attn(l, qa, ka, vt):
    b, _, s, _ = qa.shape
    tq = min(ATT_TILE, s)
    return pl.pallas_call(
        _attn_body,
        grid=(b, N_HEADS // 2, s // tq),
        in_specs=[
            pl.BlockSpec((1, 2, tq, LANES), lambda bi, gi, qi: (bi, gi, qi, 0)),
            pl.BlockSpec((1, 2, s, LANES), lambda bi, gi, qi: (bi, gi, 0, 0)),
            pl.BlockSpec((1, 2 * HEAD_DIM, s), lambda bi, gi, qi: (bi, gi, 0)),
        ],
        out_specs=pl.BlockSpec((1, tq, LANES), lambda bi, gi, qi: (bi, qi, gi)),
        out_shape=jax.ShapeDtypeStruct((b, s, D_ATT), BF16),
        scratch_shapes=[
            pltpu.VMEM((2, 2, tq, tq), F32),
            pltpu.VMEM((2, 1, tq), F32),
            pltpu.VMEM((2, 1, tq), F32),
            pltpu.VMEM((2, HEAD_DIM, tq), F32),
        ],
        compiler_params=pltpu.CompilerParams(
            dimension_semantics=("arbitrary", "arbitrary", "arbitrary"), vmem_limit_bytes=VMEM_LIMIT),
        name=f"fox_attn_l{l}",
    )(qa, ka, vt)


def _post_body(att_ref, g0_ref, part_ref, x_ref, wba_ref, wout_ref, pomg_ref, prfg_ref, wup_ref, cfw_ref,
               wdn_ref, pofg_ref, o_ref, h_carry, hbuf):
    tm = x_ref.shape[1]
    d_ff = wdn_ref.shape[1]
    n_chunks = d_ff // FF_CHUNK

    @pl.when(pl.program_id(1) == 0)
    def _():
        h_carry[...] = jnp.zeros_like(h_carry)

    y_att = jnp.dot(att_ref[0], wba_ref[0], preferred_element_type=F32)
    merged = g0_ref[0].astype(F32) * y_att + part_ref[0].astype(F32)
    mix = jnp.dot(merged.astype(BF16), wout_ref[0], preferred_element_type=F32)
    x1 = x_ref[0] + _rms(mix, pomg_ref[0])
    xn = _rms(x1, prfg_ref[0]).astype(BF16)

    def offsets(j):
        return (j * FF_CHUNK, d_ff + j * FF_CHUNK)

    slabs = FF_CHUNK // LANES

    def up_project(j):
        for half, off in enumerate(offsets(j)):
            h = jnp.dot(xn, wup_ref[0, :, off:off + FF_CHUNK], preferred_element_type=F32)
            for s in range(slabs):
                cols = slice(off + s * LANES, off + (s + 1) * LANES)
                _stage_rows(hbuf, j % 2, half * slabs + s, h[:, s * LANES:(s + 1) * LANES], h_carry, cols)

    def conv(j, half, off):
        return jnp.concatenate(
            [_staged_conv3(hbuf, j % 2, half * slabs + s, tm,
                           cfw_ref[0, :, off + s * LANES:off + (s + 1) * LANES]) for s in range(slabs)], axis=1)

    def down_project(j, gated):
        return jnp.dot(gated, wdn_ref[0, j * FF_CHUNK:(j + 1) * FF_CHUNK, :], preferred_element_type=F32)

    up_project(0)
    acc = jnp.zeros((tm, x_ref.shape[2]), F32)
    gated_prev = None
    for j in range(n_chunks):
        if j + 1 < n_chunks:
            up_project(j + 1)
        if gated_prev is not None:
            acc = acc + down_project(j - 1, gated_prev)
        off_a, off_b = offsets(j)
        gated_prev = (_gelu_tanh(conv(j, 0, off_a)) * conv(j, 1, off_b)).astype(BF16)
    acc = acc + down_project(n_chunks - 1, gated_prev)
    o_ref[0] = x1 + _rms(acc, pofg_ref[0])


def _post(l, att, g0, part, x, w_ba, w_out, post_mix_g, pre_ffn_g, w_up, conv_ffn_w, w_down, post_ffn_g):
    b, s, d = x.shape
    tm = min(ROW_TILE, s)
    d_ff = w_down.shape[1]

    def layer(shape):
        nd = len(shape)
        return pl.BlockSpec((1,) + shape, lambda bi, ti: (l,) + (0,) * nd, pipeline_mode=pl.Buffered(1))

    def rows(width):
        return pl.BlockSpec((1, tm, width), lambda bi, ti: (bi, ti, 0))

    return pl.pallas_call(
        _post_body,
        grid=(b, s // tm),
        in_specs=[
            rows(D_ATT), rows(d), rows(d), rows(d),
            layer((D_ATT, d)),
            layer((d, d)),
            layer((1, d)),
            layer((1, d)),
            layer((d, 2 * d_ff)),
            layer((CONV_WIDTH, 2 * d_ff)),
            layer((d_ff, d)),
            layer((1, d)),
        ],
        out_specs=rows(d),
        out_shape=jax.ShapeDtypeStruct((b, s, d), F32),
        scratch_shapes=[pltpu.VMEM((SUBLANES, 2 * d_ff), F32),
                        pltpu.VMEM((2, 2 * FF_CHUNK // LANES, tm + SUBLANES, LANES), F32)],
        compiler_params=pltpu.CompilerParams(
            dimension_semantics=("arbitrary", "arbitrary"), vmem_limit_bytes=VMEM_LIMIT),
        name=f"post_l{l}",
    )(att, g0, part, x, w_ba, w_out, post_mix_g, pre_ffn_g, w_up, conv_ffn_w, w_down, post_ffn_g)


def kernel(x, pre_mix_g, post_mix_g, pre_ffn_g, post_ffn_g, w_in, b_forget, b_gate, conv_mix_w, sgu_ln_g,
           sgu_ln_b, sgu_w, sgu_b, w_branch_att, w_branch_conv, w_branch_sgu, w_out, w_ffn_up, conv_ffn_w,
           w_ffn_down):
    depth, d, _ = w_in.shape
    assert d % LANES == 0 and x.shape[1] % min(ROW_TILE, x.shape[1]) == 0
    assert min(ROW_TILE, x.shape[1]) % SGU_CHUNK == 0 and FF_CHUNK % LANES == 0
    assert w_ffn_down.shape[1] % FF_CHUNK == 0

    f0 = 3 * D_ATT
    f1 = f0 + N_HEADS
    w_in_r = jnp.concatenate(
        [w_in[:, :, :f0], w_in[:, :, f1:], w_in[:, :, f0:f1],
         jnp.zeros((depth, d, LANES - N_HEADS), w_in.dtype)], axis=2).astype(BF16)
    b_forget_p = jnp.pad(b_forget, ((0, 0), (0, LANES - N_HEADS)))[:, None, :]
    b_gate_r = b_gate.reshape(depth, 1, N_BRANCHES * d)
    tril = jnp.tril(jnp.ones((SGU_CHUNK, SGU_CHUNK), sgu_w.dtype))
    sgu_w_cat = jnp.transpose(sgu_w * tril, (0, 2, 1, 3)).reshape(
        depth, SGU_CHUNK, N_SGU_GROUPS * SGU_CHUNK).astype(BF16)
    sgu_bias = jnp.repeat(jnp.transpose(sgu_b, (0, 2, 1)), HEAD_DIM, axis=2)
    row = lambda a: a[:, None, :]

    w_bc = w_branch_conv.astype(BF16)
    w_bs = w_branch_sgu.astype(BF16)
    w_ba = w_branch_att.astype(BF16)
    w_o = w_out.astype(BF16)
    w_up = w_ffn_up.astype(BF16)
    w_dn = w_ffn_down.astype(BF16)

    for l in range(depth):
        qa, ka, vt, g0, part = _in_proj(
            l, x, row(pre_mix_g), w_in_r, b_forget_p, b_gate_r, conv_mix_w, row(sgu_ln_g), row(sgu_ln_b),
            sgu_w_cat, sgu_bias, w_bc, w_bs)
        att = _fox_attn(l, qa, ka, vt)
        x = _post(l, att, g0, part, x, w_ba, w_o, row(post_mix_g), row(pre_ffn_g), w_up, conv_ffn_w, w_dn,
                  row(post_ffn_g))
    return x
```

```python
import jax
import jax.numpy as jnp
from jax import lax
from jax.experimental import pallas as pl
from jax.experimental.pallas import tpu as pltpu

F32 = jnp.float32
BF16 = jnp.bfloat16

HEAD_DIM = 64
N_HEADS = 8
D_ATT = N_HEADS * HEAD_DIM
D_CONV = 256
D_SGU = 256
N_SGU_GROUPS = 4
SGU_CHUNK = 128
CONV_WIDTH = 3
N_BRANCHES = 3
RMS_EPS = 1e-6
LN_EPS = 1e-5
LOG2E = 1.4426950408889634

LANES = 128
SUBLANES = 8
ROW_TILE = 512
ATT_TILE = 512
FF_CHUNK = 256
MASK_VALUE = -1e30
SKIP_LOG2 = 160.0
NORM_SLACK = 1.02
VMEM_LIMIT = 56 * 1024 * 1024

OFF_Q = 0
OFF_K = OFF_Q + D_ATT
OFF_V = OFF_K + D_ATT
OFF_BG = 0
OFF_CG = OFF_BG + D_CONV
OFF_HC = OFF_CG + D_CONV
OFF_U = OFF_HC + D_CONV
OFF_VS = OFF_U + D_SGU
OFF_G = OFF_VS + D_SGU

X_CT = 0
X_CS = 3 * N_HEADS


def _rms(x, g):
    return x * lax.rsqrt(jnp.mean(x * x, axis=-1, keepdims=True) + RMS_EPS) * g


def _gelu_tanh(x):
    return 0.5 * x * (1.0 + jnp.tanh(0.7978845608028654 * (x + 0.044715 * (x * x * x))))


def _sigmoid(x):
    return 1.0 / (1.0 + jnp.exp(-x))


def _log_sigmoid(x):
    return jnp.minimum(x, 0.0) - jnp.log1p(jnp.exp(-jnp.abs(x)))


def _stage_rows(buf, lead, slab, h, carry, cols):
    tm = h.shape[0]
    buf[lead, slab, 0:SUBLANES, :] = carry[:, cols]
    buf[lead, slab, SUBLANES:, :] = h
    carry[:, cols] = h[tm - SUBLANES:tm, :]


def _staged_rows(buf, lead, slab, shift, tm):
    if shift == 0:
        return buf[lead, slab, SUBLANES:tm + SUBLANES, :]
    return buf[pl.ds(lead, 1, stride=2), slab, pl.ds(SUBLANES - shift, tm), :][0]


def _staged_conv3(buf, lead, slab, tm, w):
    return (w[2:3, :] * _staged_rows(buf, lead, slab, 0, tm)
            + w[1:2, :] * _staged_rows(buf, lead, slab, 1, tm)
            + w[0:1, :] * _staged_rows(buf, lead, slab, 2, tm))


def _split3_bf16(c):
    hi = c.astype(BF16).astype(F32)
    r = c - hi
    mid = r.astype(BF16).astype(F32)
    lo = (r - mid).astype(BF16).astype(F32)
    return hi, mid, lo


def _in_proj_body(x_ref, g_ref, wqkv_ref, wf_ref, w_ref, bf_ref, bgate_ref, cw_ref, lng_ref, lnb_ref, sw_ref,
                  sb_ref, wbc_ref, wbs_ref, qa_ref, ka_ref, vt_ref, g0_ref, part_ref, stats_ref, c_carry, z_carry,
                  zbuf):
    tm = x_ref.shape[1]

    @pl.when(pl.program_id(1) == 0)
    def _():
        c_carry[...] = jnp.zeros_like(c_carry)
        z_carry[...] = jnp.zeros_like(z_carry)

    x = x_ref[0]
    xn = _rms(x, g_ref[0]).astype(BF16)

    def proj(off, width, ref=w_ref):
        return jnp.dot(xn, ref[0, :, off:off + width], preferred_element_type=F32)

    logf = _log_sigmoid(proj(0, LANES, wf_ref) + bf_ref[0])
    rows = lax.broadcasted_iota(jnp.int32, (tm, LANES), 0)
    c = logf
    sh = 1
    while sh < tm:
        c = c + jnp.where(rows >= sh, pltpu.roll(c, sh, 0), 0.0)
        sh *= 2
    c = c + c_carry[SUBLANES - 1:SUBLANES, :]
    c_carry[...] = c[tm - SUBLANES:tm, :]
    c2 = c * LOG2E
    pieces = _split3_bf16(c2)

    lane = lax.broadcasted_iota(jnp.int32, (tm, LANES), 1)
    e = lane - HEAD_DIM
    xq = jnp.zeros((tm, LANES), F32)
    xk = jnp.zeros((tm, LANES), F32)
    for i, piece in enumerate(pieces):
        lo_q = X_CT + i * N_HEADS
        lo_k = X_CS + i * N_HEADS
        xq = jnp.where((e >= lo_q) & (e < lo_q + N_HEADS), pltpu.roll(piece, HEAD_DIM + lo_q, 1), xq)
        xk = jnp.where((e >= lo_k) & (e < lo_k + N_HEADS), pltpu.roll(-piece, HEAD_DIM + lo_k, 1), xk)
    lane_row = lax.broadcasted_iota(jnp.int32, (1, LANES), 1) - HEAD_DIM

    def selector(base, h):
        hit = (lane_row == base + h) | (lane_row == base + N_HEADS + h) | (lane_row == base + 2 * N_HEADS + h)
        return jnp.where(hit, 1.0, 0.0)

    qkv = proj(0, 3 * D_ATT, wqkv_ref)
    q = qkv[:, OFF_Q:OFF_Q + D_ATT] * (HEAD_DIM ** -0.5 * LOG2E)
    k = qkv[:, OFF_K:OFF_K + D_ATT]
    vt_ref[0] = qkv[:, OFF_V:OFF_V + D_ATT].T.astype(BF16)
    head_of_row = lax.broadcasted_iota(jnp.int32, (D_ATT, LANES), 0) // HEAD_DIM
    head_sum = jnp.where(head_of_row == lax.broadcasted_iota(jnp.int32, (D_ATT, LANES), 1), 1.0, 0.0).astype(BF16)
    qn2 = jnp.dot((q * q).astype(BF16), head_sum, preferred_element_type=F32)
    kn2 = jnp.dot((k * k).astype(BF16), head_sum, preferred_element_type=F32)
    stats_ref[0, 0] = jnp.concatenate(
        [jnp.max(c2, axis=0, keepdims=True), jnp.min(c2, axis=0, keepdims=True),
         jnp.max(qn2, axis=0, keepdims=True), jnp.max(kn2, axis=0, keepdims=True),
         jnp.zeros((SUBLANES - 4, LANES), F32)], axis=0)

    for h in range(N_HEADS):
        pair = h // 2
        qp = q[:, pair * LANES:(pair + 1) * LANES]
        kp = k[:, pair * LANES:(pair + 1) * LANES]
        if h % 2 == 1:
            qp = pltpu.roll(qp, HEAD_DIM, 1)
            kp = pltpu.roll(kp, HEAD_DIM, 1)
        qa_ref[0, h] = jnp.where(lane < HEAD_DIM, qp, xq + selector(X_CS, h)).astype(BF16)
        ka_ref[0, h] = jnp.where(lane < HEAD_DIM, kp, xk + selector(X_CT, h)).astype(BF16)

    local = proj(OFF_BG, OFF_G - OFF_BG)
    z = local[:, OFF_CG:OFF_CG + D_CONV] * local[:, OFF_HC:OFF_HC + D_CONV]
    conv_slabs = []
    for s in range(D_CONV // LANES):
        cols = slice(s * LANES, (s + 1) * LANES)
        _stage_rows(zbuf, 0, s, z[:, cols], z_carry, cols)
        conv_slabs.append(_staged_conv3(zbuf, 0, s, tm, cw_ref[0, :, cols]))
    conv_pre = local[:, OFF_BG:OFF_BG + D_CONV] * jnp.concatenate(conv_slabs, axis=1)
    y_conv = jnp.dot(conv_pre.astype(BF16), wbc_ref[0], preferred_element_type=F32)

    u = _gelu_tanh(local[:, OFF_U:OFF_U + D_SGU])
    vs = _gelu_tanh(local[:, OFF_VS:OFF_VS + D_SGU])
    mu = jnp.mean(vs, axis=-1, keepdims=True)
    vc = vs - mu
    var = jnp.mean(vc * vc, axis=-1, keepdims=True)
    vln = vc * lax.rsqrt(var + LN_EPS) * lng_ref[0] + lnb_ref[0]
    grp = lax.broadcasted_iota(jnp.int32, (SGU_CHUNK, D_SGU), 1) // HEAD_DIM
    mixed = []
    for ci in range(tm // SGU_CHUNK):
        vchunk = vln[ci * SGU_CHUNK:(ci + 1) * SGU_CHUNK, :]
        stacked = jnp.concatenate(
            [jnp.where(grp == gi, vchunk, 0.0).astype(BF16) for gi in range(N_SGU_GROUPS)], axis=0)
        mixed.append(jnp.dot(sw_ref[0], stacked, preferred_element_type=F32) + sb_ref[0])
    sgu_pre = u * jnp.concatenate(mixed, axis=0)
    y_sgu = jnp.dot(sgu_pre.astype(BF16), wbs_ref[0], preferred_element_type=F32)

    d = x_ref.shape[2]

    def gate(i):
        return _sigmoid(proj(OFF_G + i * d, d) + bgate_ref[0, :, i * d:(i + 1) * d])

    g0_ref[0] = gate(0).astype(BF16)
    part_ref[0] = (gate(1) * y_conv + gate(2) * y_sgu).astype(BF16)


def _in_proj(l, x, pre_mix_g, w_qkv, w_f, w_rest, b_forget_p, b_gate_r, conv_mix_w, sgu_ln_g, sgu_ln_b,
             sgu_w_cat, sgu_bias, w_bc, w_bs):
    b, s, d = x.shape
    tm = min(ROW_TILE, s)

    def layer(shape):
        nd = len(shape)
        return pl.BlockSpec((1,) + shape, lambda bi, ti: (l,) + (0,) * nd, pipeline_mode=pl.Buffered(1))

    return pl.pallas_call(
        _in_proj_body,
        grid=(b, s // tm),
        in_specs=[
            pl.BlockSpec((1, tm, d), lambda bi, ti: (bi, ti, 0)),
            layer((1, d)),
            layer((d, w_qkv.shape[2])),
            layer((d, LANES)),
            layer((d, w_rest.shape[2])),
            layer((1, LANES)),
            layer((1, N_BRANCHES * d)),
            layer((CONV_WIDTH, D_CONV)),
            layer((1, D_SGU)),
            layer((1, D_SGU)),
            layer((SGU_CHUNK, N_SGU_GROUPS * SGU_CHUNK)),
            layer((SGU_CHUNK, D_SGU)),
            layer((D_CONV, d)),
            layer((D_SGU, d)),
        ],
        out_specs=[
            pl.BlockSpec((1, N_HEADS, tm, LANES), lambda bi, ti: (bi, 0, ti, 0)),
            pl.BlockSpec((1, N_HEADS, tm, LANES), lambda bi, ti: (bi, 0, ti, 0)),
            pl.BlockSpec((1, D_ATT, tm), lambda bi, ti: (bi, 0, ti)),
            pl.BlockSpec((1, tm, d), lambda bi, ti: (bi, ti, 0)),
            pl.BlockSpec((1, tm, d), lambda bi, ti: (bi, ti, 0)),
            pl.BlockSpec((1, 1, SUBLANES, LANES), lambda bi, ti: (bi, ti, 0, 0)),
        ],
        out_shape=[
            jax.ShapeDtypeStruct((b, N_HEADS, s, LANES), BF16),
            jax.ShapeDtypeStruct((b, N_HEADS, s, LANES), BF16),
            jax.ShapeDtypeStruct((b, D_ATT, s), BF16),
            jax.ShapeDtypeStruct((b, s, d), BF16),
            jax.ShapeDtypeStruct((b, s, d), BF16),
            jax.ShapeDtypeStruct((b, s // tm, SUBLANES, LANES), F32),
        ],
        scratch_shapes=[pltpu.VMEM((SUBLANES, LANES), F32), pltpu.VMEM((SUBLANES, D_CONV), F32),
                        pltpu.VMEM((1, D_CONV // LANES, tm + SUBLANES, LANES), F32)],
        compiler_params=pltpu.CompilerParams(
            dimension_semantics=("arbitrary", "arbitrary"), vmem_limit_bytes=VMEM_LIMIT),
        name=f"in_proj_l{l}",
    )(x, pre_mix_g, w_qkv, w_f, w_rest, b_forget_p, b_gate_r, conv_mix_w, sgu_ln_g, sgu_ln_b, sgu_w_cat, sgu_bias,
      w_bc, w_bs)


def _attn_body(j0_ref, q_ref, k_ref, vt_ref, o_ref, s_buf, m_sc, l_sc, acc_sc):
    tq = q_ref.shape[2]
    qi = pl.program_id(2)
    m_sc[...] = jnp.full_like(m_sc, MASK_VALUE)
    l_sc[...] = jnp.zeros_like(l_sc)
    acc_sc[...] = jnp.zeros_like(acc_sc)

    def scores(j, slot):
        kv = pl.ds(pl.multiple_of(j * tq, tq), tq)
        for hh in range(2):
            s_buf[slot, hh] = lax.dot_general(k_ref[0, hh, kv, :], q_ref[0, hh], (((1,), (1,)), ((), ())),
                                              preferred_element_type=F32)

    def update(j, slot, masked):
        kv = pl.ds(pl.multiple_of(j * tq, tq), tq)
        for hh in range(2):
            s_t = s_buf[slot, hh]
            if masked:
                key = lax.broadcasted_iota(jnp.int32, (tq, tq), 0)
                qry = lax.broadcasted_iota(jnp.int32, (tq, tq), 1)
                s_t = jnp.where(key <= qry, s_t, MASK_VALUE)
            m_old = m_sc[hh]
            m_new = jnp.maximum(m_old, jnp.max(s_t, axis=0, keepdims=True))
            alpha = jnp.exp2(m_old - m_new)
            p_t = jnp.exp2(s_t - m_new)
            l_sc[hh] = alpha * l_sc[hh] + jnp.sum(p_t, axis=0, keepdims=True)
            v_t = vt_ref[0, hh * HEAD_DIM:(hh + 1) * HEAD_DIM, kv]
            acc_sc[hh] = alpha * acc_sc[hh] + jnp.dot(v_t, p_t.astype(BF16), preferred_element_type=F32)
            m_sc[hh] = m_new

    n_q = pl.num_programs(2)
    j0 = j0_ref[(pl.program_id(0) * pl.num_programs(1) + pl.program_id(1)) * n_q + qi]
    n_plain = qi - j0
    scores(j0, 0)

    def pair(i, carry):
        j = j0 + 2 * i
        scores(j + 1, 1)
        update(j, 0, False)
        scores(j + 2, 0)
        update(j + 1, 1, False)
        return carry

    lax.fori_loop(0, lax.shift_right_logical(n_plain, 1), pair, 0)
    odd = lax.bitwise_and(n_plain, 1) == 1

    @pl.when(odd)
    def _():
        scores(qi, 1)
        update(qi - 1, 0, False)
        update(qi, 1, True)

    @pl.when(jnp.logical_not(odd))
    def _():
        update(qi, 0, True)

    o_t = jnp.concatenate([acc_sc[0] / l_sc[0], acc_sc[1] / l_sc[1]], axis=0)
    o_ref[0] = o_t.T.astype(BF16)


def _first_needed_tile(stats):
    c_max, c_min, qn2, kn2 = (stats[:, :, r, :N_HEADS] for r in range(4))
    u = (NORM_SLACK * jnp.sqrt(qn2[:, :, None, :] * kn2[:, None, :, :])
         + c_max[:, :, None, :] - c_min[:, None, :, :])
    d = NORM_SLACK * jnp.sqrt(qn2 * kn2)[:, :, None, :]
    n_t = stats.shape[1]
    tile = jnp.arange(n_t)
    needed = (u + d >= -SKIP_LOG2) | (tile[:, None] == tile[None, :])[None, :, :, None]
    needed = needed.reshape(needed.shape[:3] + (N_HEADS // 2, 2)).any(axis=-1)
    first = jnp.argmax(needed, axis=2).astype(jnp.int32)
    return jnp.transpose(first, (0, 2, 1)).reshape(-1)


def _fox_attn(l, qa, ka, vt, stats):
    b, _, s, _ = qa.shape
    tq = min(ATT_TILE, s)
    assert stats.shape[1] == s // tq
    return pl.pallas_call(
        _attn_body,
        grid_spec=pltpu.PrefetchScalarGridSpec(
            num_scalar_prefetch=1,
            grid=(b, N_HEADS // 2, s // tq),
            in_specs=[
                pl.BlockSpec((1, 2, tq, LANES), lambda bi, gi, qi, j0: (bi, gi, qi, 0)),
                pl.BlockSpec((1, 2, s, LANES), lambda bi, gi, qi, j0: (bi, gi, 0, 0)),
                pl.BlockSpec((1, 2 * HEAD_DIM, s), lambda bi, gi, qi, j0: (bi, gi, 0)),
            ],
            out_specs=pl.BlockSpec((1, tq, LANES), lambda bi, gi, qi, j0: (bi, qi, gi)),
            scratch_shapes=[
                pltpu.VMEM((2, 2, tq, tq), F32),
                pltpu.VMEM((2, 1, tq), F32),
                pltpu.VMEM((2, 1, tq), F32),
                pltpu.VMEM((2, HEAD_DIM, tq), F32),
            ],
        ),
        out_shape=jax.ShapeDtypeStruct((b, s, D_ATT), BF16),
        compiler_params=pltpu.CompilerParams(
            dimension_semantics=("arbitrary", "arbitrary", "arbitrary"), vmem_limit_bytes=VMEM_LIMIT),
        name=f"fox_attn_l{l}",
    )(_first_needed_tile(stats), qa, ka, vt)


def _post_body(att_ref, g0_ref, part_ref, x_ref, wba_ref, wout_ref, pomg_ref, prfg_ref, wup_ref, cfw_ref,
               wdn_ref, pofg_ref, o_ref, h_carry, hbuf):
    tm = x_ref.shape[1]
    d_ff = wdn_ref.shape[1]
    n_chunks = d_ff // FF_CHUNK

    @pl.when(pl.program_id(1) == 0)
    def _():
        h_carry[...] = jnp.zeros_like(h_carry)

    y_att = jnp.dot(att_ref[0], wba_ref[0], preferred_element_type=F32)
    merged = g0_ref[0].astype(F32) * y_att + part_ref[0].astype(F32)
    mix = jnp.dot(merged.astype(BF16), wout_ref[0], preferred_element_type=F32)
    x1 = x_ref[0] + _rms(mix, pomg_ref[0])
    xn = _rms(x1, prfg_ref[0]).astype(BF16)

    def offsets(j):
        return (j * FF_CHUNK, d_ff + j * FF_CHUNK)

    slabs = FF_CHUNK // LANES

    def up_project(j):
        for half, off in enumerate(offsets(j)):
            h = jnp.dot(xn, wup_ref[0, :, off:off + FF_CHUNK], preferred_element_type=F32)
            for s in range(slabs):
                cols = slice(off + s * LANES, off + (s + 1) * LANES)
                _stage_rows(hbuf, j % 2, half * slabs + s, h[:, s * LANES:(s + 1) * LANES], h_carry, cols)

    def conv(j, half, off):
        return jnp.concatenate(
            [_staged_conv3(hbuf, j % 2, half * slabs + s, tm,
                           cfw_ref[0, :, off + s * LANES:off + (s + 1) * LANES]) for s in range(slabs)], axis=1)

    def down_project(j, gated):
        return jnp.dot(gated, wdn_ref[0, j * FF_CHUNK:(j + 1) * FF_CHUNK, :], preferred_element_type=F32)

    up_project(0)
    acc = jnp.zeros((tm, x_ref.shape[2]), F32)
    gated_prev = None
    for j in range(n_chunks):
        if j + 1 < n_chunks:
            up_project(j + 1)
        if gated_prev is not None:
            acc = acc + down_project(j - 1, gated_prev)
        off_a, off_b = offsets(j)
        gated_prev = (_gelu_tanh(conv(j, 0, off_a)) * conv(j, 1, off_b)).astype(BF16)
    acc = acc + down_project(n_chunks - 1, gated_prev)
    o_ref[0] = x1 + _rms(acc, pofg_ref[0])


def _post(l, att, g0, part, x, w_ba, w_out, post_mix_g, pre_ffn_g, w_up, conv_ffn_w, w_down, post_ffn_g):
    b, s, d = x.shape
    tm = min(ROW_TILE, s)
    d_ff = w_down.shape[1]

    def layer(shape):
        nd = len(shape)
        return pl.BlockSpec((1,) + shape, lambda bi, ti: (l,) + (0,) * nd, pipeline_mode=pl.Buffered(1))

    def rows(width):
        return pl.BlockSpec((1, tm, width), lambda bi, ti: (bi, ti, 0))

    return pl.pallas_call(
        _post_body,
        grid=(b, s // tm),
        in_specs=[
            rows(D_ATT), rows(d), rows(d), rows(d),
            layer((D_ATT, d)),
            layer((d, d)),
            layer((1, d)),
            layer((1, d)),
            layer((d, 2 * d_ff)),
            layer((CONV_WIDTH, 2 * d_ff)),
            layer((d_ff, d)),
            layer((1, d)),
        ],
        out_specs=rows(d),
        out_shape=jax.ShapeDtypeStruct((b, s, d), F32),
        scratch_shapes=[pltpu.VMEM((SUBLANES, 2 * d_ff), F32),
                        pltpu.VMEM((2, 2 * FF_CHUNK // LANES, tm + SUBLANES, LANES), F32)],
        compiler_params=pltpu.CompilerParams(
            dimension_semantics=("arbitrary", "arbitrary"), vmem_limit_bytes=VMEM_LIMIT),
        name=f"post_l{l}",
    )(att, g0, part, x, w_ba, w_out, post_mix_g, pre_ffn_g, w_up, conv_ffn_w, w_down, post_ffn_g)


def kernel(x, pre_mix_g, post_mix_g, pre_ffn_g, post_ffn_g, w_in, b_forget, b_gate, conv_mix_w, sgu_ln_g,
           sgu_ln_b, sgu_w, sgu_b, w_branch_att, w_branch_conv, w_branch_sgu, w_out, w_ffn_up, conv_ffn_w,
           w_ffn_down):
    depth, d, _ = w_in.shape
    assert d % LANES == 0 and x.shape[1] % min(ROW_TILE, x.shape[1]) == 0
    assert min(ROW_TILE, x.shape[1]) % SGU_CHUNK == 0 and FF_CHUNK % LANES == 0
    assert w_ffn_down.shape[1] % FF_CHUNK == 0 and ROW_TILE == ATT_TILE

    f0 = 3 * D_ATT
    f1 = f0 + N_HEADS
    w_qkv = w_in[:, :, :f0].astype(BF16)
    w_f = jnp.pad(w_in[:, :, f0:f1], ((0, 0), (0, 0), (0, LANES - N_HEADS))).astype(BF16)
    w_rest = w_in[:, :, f1:].astype(BF16)
    b_forget_p = jnp.pad(b_forget, ((0, 0), (0, LANES - N_HEADS)))[:, None, :]
    b_gate_r = b_gate.reshape(depth, 1, N_BRANCHES * d)
    tril = jnp.tril(jnp.ones((SGU_CHUNK, SGU_CHUNK), sgu_w.dtype))
    sgu_w_cat = jnp.transpose(sgu_w * tril, (0, 2, 1, 3)).reshape(
        depth, SGU_CHUNK, N_SGU_GROUPS * SGU_CHUNK).astype(BF16)
    sgu_bias = jnp.repeat(jnp.transpose(sgu_b, (0, 2, 1)), HEAD_DIM, axis=2)
    row = lambda a: a[:, None, :]

    w_bc = w_branch_conv.astype(BF16)
    w_bs = w_branch_sgu.astype(BF16)
    w_ba = w_branch_att.astype(BF16)
    w_o = w_out.astype(BF16)
    w_up = w_ffn_up.astype(BF16)
    w_dn = w_ffn_down.astype(BF16)

    for l in range(depth):
        qa, ka, vt, g0, part, stats = _in_proj(
            l, x, row(pre_mix_g), w_qkv, w_f, w_rest, b_forget_p, b_gate_r, conv_mix_w, row(sgu_ln_g),
            row(sgu_ln_b), sgu_w_cat, sgu_bias, w_bc, w_bs)
        att = _fox_attn(l, qa, ka, vt, stats)
        x = _post(l, att, g0, part, x, w_ba, w_o, row(post_mix_g), row(pre_ffn_g), w_up, conv_ffn_w, w_dn,
                  row(post_ffn_g))
    return x
```
